```python
import jax, jax.numpy as jnp
from jax import lax
import numpy as np

D_MODEL = 4096
BATCH = 32
SEQ = 256
DEPTH = 2
DEC_BATCH = 4
DEC_SEQ = 1024
PAST_LEN = 256

GRID_W = 64
EPS = 1e-6
N_FOURIER_GROUPS = 4
FOURIER_GROUP = D_MODEL // 16
D_FOURIER = N_FOURIER_GROUPS * FOURIER_GROUP
POOL_WINDOWS = (2, 4, 8, 16)
POOL_GROUP = D_MODEL // 16
D_POOL = len(POOL_WINDOWS) * POOL_GROUP
N_HEADS = 16
NOPE_DIM = 128
ROPE_DIM = 64
V_DIM = 128
QK_DIM = NOPE_DIM + ROPE_DIM
Q_LORA = D_MODEL // 4
KV_LORA = 512
ROPE_BASE = 10000.0
SM_SCALE = QK_DIM ** -0.5
Q_BLOCK = 128
N_BRANCHES = 3
N_IN = D_FOURIER + D_POOL + Q_LORA + KV_LORA + ROPE_DIM + N_BRANCHES * D_MODEL
SPLITS = (D_FOURIER,
          D_FOURIER + D_POOL,
          D_FOURIER + D_POOL + Q_LORA,
          D_FOURIER + D_POOL + Q_LORA + KV_LORA,
          D_FOURIER + D_POOL + Q_LORA + KV_LORA + ROPE_DIM)
D_FF = ((8 * D_MODEL + 3 * 256 - 1) // (3 * 256)) * 256

kernel_name = 'gated_hybrid_fourier_pool_mla_dit_step'


def rms_norm(x, g):
    xf = x.astype(jnp.float32)
    y = xf * lax.rsqrt(jnp.mean(xf * xf, axis=-1, keepdims=True) + EPS)
    return (y * g.astype(jnp.float32)).astype(x.dtype)


def grid_angles(n_tokens):
    rows = n_tokens // GRID_W
    row = jnp.repeat(jnp.arange(rows, dtype=jnp.float32), GRID_W)
    col = jnp.tile(jnp.arange(GRID_W, dtype=jnp.float32), rows)
    inv = ROPE_BASE ** (-jnp.arange(ROPE_DIM // 4, dtype=jnp.float32) * (4.0 / ROPE_DIM))
    return row[:, None] * inv[None, :], col[:, None] * inv[None, :]


def rope_1d(x, ang):
    x1, x2 = jnp.split(x, 2, axis=-1)
    cos = jnp.cos(ang).astype(x.dtype)
    sin = jnp.sin(ang).astype(x.dtype)
    return jnp.concatenate([x1 * cos - x2 * sin, x1 * sin + x2 * cos], axis=-1)


def axial_rope(x, ang_row, ang_col):
    xr, xc = jnp.split(x, 2, axis=-1)
    return jnp.concatenate([rope_1d(xr, ang_row), rope_1d(xc, ang_col)], axis=-1)


def fourier_branch(f, w_fourier):
    B, S, _ = f.shape
    fg = f.astype(jnp.float32).reshape(B, S, N_FOURIER_GROUPS, FOURIER_GROUP)
    mixed = jnp.fft.fftn(fg, axes=(1, 3), norm='ortho').real
    return mixed.astype(f.dtype).reshape(B, S, D_FOURIER) @ w_fourier


def pool_branch(p, pool_scale, w_pool):
    B, S, _ = p.shape
    pg = p.astype(jnp.float32).reshape(B, S, len(POOL_WINDOWS), POOL_GROUP)
    cs = jnp.pad(jnp.cumsum(pg, axis=1), ((0, 0), (1, 0), (0, 0), (0, 0)))
    t = jnp.arange(S)
    outs = []
    for gi, w in enumerate(POOL_WINDOWS):
        lo = jnp.clip(t - w // 2, 0, S)
        hi = jnp.clip(t + w - w // 2, 0, S)
        csg = cs[:, :, gi]
        win_sum = jnp.take(csg, hi, axis=1) - jnp.take(csg, lo, axis=1)
        cnt = (hi - lo).astype(jnp.float32)[None, :, None]
        outs.append(win_sum / cnt - pg[:, :, gi])
    pooled = jnp.stack(outs, axis=2) * pool_scale.astype(jnp.float32).reshape(len(POOL_WINDOWS), POOL_GROUP)
    return jnp.einsum('bsgc,gcd->bsd', pooled.astype(p.dtype), w_pool)


def mla_attention(q_nope, q_pe, k_nope, k_pe, v):
    B, Sq, H, _ = q_nope.shape
    qb = min(Q_BLOCK, Sq)
    nb = Sq // qb

    def to_blocks(t):
        return jnp.moveaxis(t.reshape(B, nb, qb, H, t.shape[-1]), 1, 0)

    def attend(blk):
        qn, qp = blk
        s = (jnp.einsum('bqhd,bkhd->bhqk', qn, k_nope, preferred_element_type=jnp.float32)
             + jnp.einsum('bqhr,bkr->bhqk', qp, k_pe, preferred_element_type=jnp.float32)) * SM_SCALE
        prob = jax.nn.softmax(s, axis=-1).astype(v.dtype)
        return jnp.einsum('bhqk,bkhd->bqhd', prob, v)

    o = lax.map(attend, (to_blocks(q_nope), to_blocks(q_pe)))
    return jnp.moveaxis(o, 0, 1).reshape(B, Sq, H, V_DIM)


def mla_branch(cq, ckv, k_pe, angles, ctx_ckv, ctx_kpe, g_q, w_q_b, g_kv, w_kv_b, w_o_mla):
    B, S, _ = cq.shape
    q = (rms_norm(cq, g_q) @ w_q_b).reshape(B, S, N_HEADS, QK_DIM)
    q_nope, q_pe = q[..., :NOPE_DIM], q[..., NOPE_DIM:]
    c_kv = rms_norm(ckv, g_kv)
    if angles is not None:
        ang_r, ang_c = angles
        q_pe = axial_rope(q_pe, ang_r[:, None, :], ang_c[:, None, :])
        k_pe = axial_rope(k_pe, ang_r, ang_c)
    if ctx_ckv is None:
        key_c, key_pe = c_kv, k_pe
    else:
        key_c = jnp.concatenate([ctx_ckv, c_kv], axis=1)
        key_pe = jnp.concatenate([ctx_kpe, k_pe], axis=1)
    Sk = key_c.shape[1]
    kv = (key_c @ w_kv_b).reshape(B, Sk, N_HEADS, NOPE_DIM + V_DIM)
    k_nope, v = kv[..., :NOPE_DIM], kv[..., NOPE_DIM:]
    o = mla_attention(q_nope, q_pe, k_nope, key_pe, v)
    return o.reshape(B, S, N_HEADS * V_DIM) @ w_o_mla, c_kv, k_pe


def token_mixer(h, angles, ctx_ckv, ctx_kpe, w_in, w_fourier, pool_scale, w_pool,
                g_q, w_q_b, g_kv, w_kv_b, w_o_mla, w_out):
    B, S, _ = h.shape
    z = h @ w_in
    f, p, cq, ckv, k_pe, gates = jnp.split(z, SPLITS, axis=-1)
    y_f = fourier_branch(f, w_fourier)
    y_p = pool_branch(p, pool_scale, w_pool)
    y_a, c_kv, k_pe = mla_branch(cq, ckv, k_pe, angles, ctx_ckv, ctx_kpe, g_q, w_q_b, g_kv, w_kv_b, w_o_mla)
    g = jax.nn.sigmoid(gates.astype(jnp.float32)).astype(h.dtype).reshape(B, S, N_BRANCHES, D_MODEL)
    merged = g[:, :, 0] * y_f + g[:, :, 1] * y_p + g[:, :, 2] * y_a
    return merged @ w_out, c_kv, k_pe


def swiglu(h, w_gate, w_up, w_down):
    return (jax.nn.silu(h @ w_gate) * (h @ w_up)) @ w_down


def trunk_layer(x, cond, angles, ctx_ckv, ctx_kpe, w_ada, b_ada, g_norm1, g_norm2, w_in, w_fourier,
                pool_scale, w_pool, g_q, w_q_b, g_kv, w_kv_b, w_o_mla, w_out, w_gate, w_up, w_down):
    mod = jax.nn.silu(cond) @ w_ada + b_ada
    shift1, scale1, gate1, shift2, scale2, gate2 = jnp.split(mod[:, None, :], 6, axis=-1)
    h = rms_norm(x, g_norm1) * (1 + scale1) + shift1
    mix, c_kv, k_pe = token_mixer(h, angles, ctx_ckv, ctx_kpe, w_in, w_fourier, pool_scale, w_pool,
                                  g_q, w_q_b, g_kv, w_kv_b, w_o_mla, w_out)
    x = x + gate1 * mix
    h = rms_norm(x, g_norm2) * (1 + scale2) + shift2
    x = x + gate2 * swiglu(h, w_gate, w_up, w_down)
    return x, c_kv, k_pe


def setup_inputs(seed: int = 0) -> dict:
    key = jax.random.key(seed)
    ks = jax.random.split(key, 26)
    f32 = jnp.float32

    def normal(k, shape, scale=1.0):
        return jax.random.normal(k, shape, f32) * scale

    def gain(k, shape):
        return 1.0 + 0.02 * jax.random.normal(k, shape, f32)

    L = DEPTH
    return {
        'x_prompt': normal(ks[0], (BATCH, SEQ, D_MODEL)),
        'x_sample': normal(ks[1], (DEC_BATCH, DEC_SEQ, D_MODEL)),
        'cache_ckv': normal(ks[2], (DEC_BATCH, DEPTH, PAST_LEN, KV_LORA)),
        'cache_krope': normal(ks[3], (DEC_BATCH, DEPTH, PAST_LEN, ROPE_DIM)),
        'c': normal(ks[4], (DEC_BATCH, D_MODEL)),
        'c_ctx': normal(ks[5], (D_MODEL,)),
        'w_ada': normal(ks[6], (L, D_MODEL, 6 * D_MODEL), D_MODEL ** -0.5),
        'b_ada': normal(ks[7], (L, 6 * D_MODEL), 0.02),
        'g_norm1': gain(ks[8], (L, D_MODEL)),
        'g_norm2': gain(ks[9], (L, D_MODEL)),
        'w_in': normal(ks[10], (L, D_MODEL, N_IN), D_MODEL ** -0.5),
        'w_fourier': normal(ks[11], (L, D_FOURIER, D_MODEL), D_FOURIER ** -0.5),
        'pool_scale': gain(ks[12], (L, D_POOL)),
        'w_pool': normal(ks[13], (L, len(POOL_WINDOWS), POOL_GROUP, D_MODEL), D_POOL ** -0.5),
        'g_q': gain(ks[14], (L, Q_LORA)),
        'w_q_b': normal(ks[15], (L, Q_LORA, N_HEADS * QK_DIM), Q_LORA ** -0.5),
        'g_kv': gain(ks[16], (L, KV_LORA)),
        'w_kv_b': normal(ks[17], (L, KV_LORA, N_HEADS * (NOPE_DIM + V_DIM)), KV_LORA ** -0.5),
        'w_o_mla': normal(ks[18], (L, N_HEADS * V_DIM, D_MODEL), (N_HEADS * V_DIM) ** -0.5),
        'w_out': normal(ks[19], (L, D_MODEL, D_MODEL), D_MODEL ** -0.5),
        'w_gate': normal(ks[20], (L, D_MODEL, D_FF), D_MODEL ** -0.5),
        'w_up': normal(ks[21], (L, D_MODEL, D_FF), D_MODEL ** -0.5),
        'w_down': normal(ks[22], (L, D_FF, D_MODEL), D_FF ** -0.5),
        'g_final': gain(ks[23], (D_MODEL,)),
    }


def reference(x_prompt, x_sample, cache_ckv, cache_krope, c, c_ctx, w_ada, b_ada, g_norm1, g_norm2,
              w_in, w_fourier, pool_scale, w_pool, g_q, w_q_b, g_kv, w_kv_b, w_o_mla, w_out,
              w_gate, w_up, w_down, g_final):
    layer_params = (w_ada, b_ada, g_norm1, g_norm2, w_in, w_fourier, pool_scale, w_pool,
                    g_q, w_q_b, g_kv, w_kv_b, w_o_mla, w_out, w_gate, w_up, w_down)

    xp = x_prompt
    cond_ctx = c_ctx[None, :]
    ckv_layers, kpe_layers = [], []
    for l in range(DEPTH):
        lw = [prm[l] for prm in layer_params]
        xp, ckv_l, kpe_l = trunk_layer(xp, cond_ctx, None, None, None, *lw)
        ckv_layers.append(ckv_l)
        kpe_layers.append(kpe_l)
    y_prompt = rms_norm(xp, g_final)
    new_ckv = jnp.stack(ckv_layers, axis=1)
    new_krope = jnp.stack(kpe_layers, axis=1)

    angles = grid_angles(x_sample.shape[1])
    xs = x_sample
    for l in range(DEPTH):
        lw = [prm[l] for prm in layer_params]
        xs, _, _ = trunk_layer(xs, c, angles, cache_ckv[:, l], cache_krope[:, l], *lw)
    y_sample = rms_norm(xs, g_final)

    return (y_prompt, y_sample, new_ckv, new_krope)
```

```python
import functools

import numpy as np
import jax
import jax.numpy as jnp
from jax.experimental import pallas as pl
from jax.experimental.pallas import tpu as pltpu

N_HEADS = 16
NOPE_DIM = 128
ROPE_DIM = 64
V_DIM = 128
HEAD_PAD = 256
GRID_W = 64
EPS = 1e-6
ROPE_BASE = 10000.0
SM_SCALE = (NOPE_DIM + ROPE_DIM) ** -0.5
POOL_WINDOWS = (2, 4, 8, 16)
N_FOURIER_GROUPS = 4
MOD_ROWS = 8
LANE = 128

BF16 = jnp.bfloat16
F32 = jnp.float32


def _params(n_grid, vmem_mb):
    return pltpu.CompilerParams(dimension_semantics=("arbitrary",) * n_grid,
                                vmem_limit_bytes=vmem_mb * 2 ** 20)


def _dot(a, b):
    return jnp.dot(a, b, preferred_element_type=F32)


def _dot_nt(a, b):
    return jax.lax.dot_general(a, b, (((1,), (1,)), ((), ())), preferred_element_type=F32)


SLAB_ROWS = 256


def _slabs(n_rows):
    r = int(np.gcd(SLAB_ROWS, n_rows))
    return [slice(s * r, (s + 1) * r) for s in range(n_rows // r)]


def _mod_row(i, bm, m_ctx, tch):
    start = i * bm
    return jnp.where(start < m_ctx, 0, 1 + (start - m_ctx) // tch)


def _ada_kernel(c_ref, w_ref, b_ref, o_ref):
    c = c_ref[...]
    a = (c * jax.nn.sigmoid(c)).astype(BF16)
    o_ref[0] = _dot(a, w_ref[0].astype(BF16)) + b_ref[0]


def _ada_mod(cond, w_ada, b_ada):
    L, D, N = w_ada.shape
    bn = 512
    return pl.pallas_call(
        _ada_kernel,
        grid=(L, N // bn),
        in_specs=[pl.BlockSpec((MOD_ROWS, D), lambda l, j: (0, 0)),
                  pl.BlockSpec((1, D, bn), lambda l, j: (l, 0, j)),
                  pl.BlockSpec((1, 1, bn), lambda l, j: (l, 0, j))],
        out_specs=pl.BlockSpec((1, MOD_ROWS, bn), lambda l, j: (l, 0, j)),
        out_shape=jax.ShapeDtypeStruct((L, MOD_ROWS, N), F32),
        compiler_params=_params(2, 40),
        name="ada_mod",
    )(cond, w_ada, b_ada.reshape(L, 1, N))


def _row_specs(xs, bm, bn, n_grid):
    idx = (lambda *g: (g[-1], g[0])) if n_grid == 2 else (lambda *g: (g[-1], 0))
    if len(xs) == 1:
        return [pl.BlockSpec((bm, bn), idx)]
    n_a = xs[0].shape[0] // bm
    last_b = xs[1].shape[0] // bm - 1
    spec_a = pl.BlockSpec((bm, bn), lambda *g: (jnp.minimum(idx(*g)[0], n_a - 1), idx(*g)[1]))
    spec_b = pl.BlockSpec((bm, bn), lambda *g: (jnp.clip(idx(*g)[0] - n_a, 0, last_b), idx(*g)[1]))
    return [spec_a, spec_b]


def _norm_mod_kernel(*refs, n_x, n_a):
    x_refs, (g_ref, sh_ref, sc_ref, o_ref) = refs[:n_x], refs[n_x:]
    x = x_refs[0][...] if n_x == 1 else jnp.where(pl.program_id(0) < n_a, x_refs[0][...], x_refs[1][...])
    r = jax.lax.rsqrt(jnp.mean(x * x, axis=-1, keepdims=True) + EPS)
    y = x * r * g_ref[...]
    o_ref[...] = (y * (1.0 + sc_ref[0, 0]) + sh_ref[0, 0]).astype(o_ref.dtype)


def _norm_mod(xs, g, mod4, layer, shift_blk, dims):
    M = sum(x.shape[0] for x in xs)
    D = xs[0].shape[1]
    bm = min(256, dims["tch"])
    row = functools.partial(_mod_row, bm=bm, m_ctx=dims["m_ctx"], tch=dims["tch"])
    return pl.pallas_call(
        functools.partial(_norm_mod_kernel, n_x=len(xs), n_a=xs[0].shape[0] // bm),
        grid=(M // bm,),
        in_specs=_row_specs(xs, bm, D, 1) + [
            pl.BlockSpec((1, D), lambda i: (0, 0)),
            pl.BlockSpec((1, 1, 1, D), lambda i: (layer, row(i), 0, shift_blk)),
            pl.BlockSpec((1, 1, 1, D), lambda i: (layer, row(i), 0, shift_blk + 1))],
        out_specs=pl.BlockSpec((bm, D), lambda i: (i, 0)),
        out_shape=jax.ShapeDtypeStruct((M, D), BF16),
        compiler_params=_params(1, 32),
        name="norm_mod",
    )(*xs, g, mod4, mod4)


def _final_norm_kernel(x_ref, g_ref, o_ref):
    x = x_ref[...]
    r = jax.lax.rsqrt(jnp.mean(x * x, axis=-1, keepdims=True) + EPS)
    o_ref[...] = x * r * g_ref[...]


def _final_norm(x, g, row0, rows):
    D = x.shape[1]
    bm = int(np.gcd(256, np.gcd(rows, row0 if row0 else rows)))
    blk0 = row0 // bm
    return pl.pallas_call(
        _final_norm_kernel,
        grid=(rows // bm,),
        in_specs=[pl.BlockSpec((bm, D), lambda i: (blk0 + i, 0)),
                  pl.BlockSpec((1, D), lambda i: (0, 0))],
        out_specs=pl.BlockSpec((bm, D), lambda i: (i, 0)),
        out_shape=jax.ShapeDtypeStruct((rows, D), F32),
        compiler_params=_params(1, 32),
        name="final_norm",
    )(x, g.reshape(1, D))


IN_BN = 512


def _in_proj_kernel(h_ref, w_ref, z_ref, kpe_ref, g_ref, *, n_small_blk):
    j = pl.program_id(1)

    @pl.when(j < n_small_blk)
    def _():
        w = w_ref[0].astype(BF16)
        for rows in _slabs(h_ref.shape[0]):
            z_ref[rows, :] = _dot(h_ref[rows, :], w)

    @pl.when(j == n_small_blk)
    def _():
        kpe_ref[...] = _dot(h_ref[...], w_ref[0, :, :LANE].astype(BF16))

    @pl.when(j >= n_small_blk)
    def _():
        w = w_ref[0].astype(BF16)
        for rows in _slabs(h_ref.shape[0]):
            g_ref[rows, :] = jax.nn.sigmoid(_dot(h_ref[rows, :], w)).astype(g_ref.dtype)


def _in_proj(h, w_in, layer, off_pe, bm):
    M, K = h.shape
    N = w_in.shape[2]
    bm = min(bm, M)
    n_small_blk = off_pe // IN_BN
    nb = -(-N // IN_BN)
    n_gate_blk = nb - n_small_blk
    return pl.pallas_call(
        functools.partial(_in_proj_kernel, n_small_blk=n_small_blk),
        grid=(M // bm, nb),
        in_specs=[pl.BlockSpec((bm, K), lambda i, j: (i, 0)),
                  pl.BlockSpec((1, K, IN_BN), lambda i, j: (layer, 0, j))],
        out_specs=[pl.BlockSpec((bm, IN_BN), lambda i, j: (i, jnp.minimum(j, n_small_blk - 1))),
                   pl.BlockSpec((bm, LANE), lambda i, j: (i, 0)),
                   pl.BlockSpec((bm, IN_BN), lambda i, j: (i, jnp.maximum(j - n_small_blk, 0)))],
        out_shape=[jax.ShapeDtypeStruct((M, off_pe), F32),
                   jax.ShapeDtypeStruct((M, LANE), F32),
                   jax.ShapeDtypeStruct((M, n_gate_blk * IN_BN), BF16)],
        compiler_params=_params(2, 56),
        name="in_proj",
    )(h, w_in)


def _mm_resid_kernel(a_ref, w_ref, *refs, n_x, n_a):
    x_refs, (gate_ref, o_ref) = refs[:n_x], refs[n_x:]
    first = pl.program_id(1) < n_a
    for rows in _slabs(a_ref.shape[0]):
        x = x_refs[0][rows, :] if n_x == 1 else jnp.where(first, x_refs[0][rows, :], x_refs[1][rows, :])
        o_ref[rows, :] = x + gate_ref[0, 0] * _dot(a_ref[rows, :], w_ref[...])


def _mm_resid(a, w, xs, mod4, layer, gate_blk, bm, bn, dims, vmem_mb, name):
    M, K = a.shape
    N = w.shape[1]
    bm = min(bm, dims["tch"])
    row = functools.partial(_mod_row, bm=bm, m_ctx=dims["m_ctx"], tch=dims["tch"])
    nb = N // bn
    return pl.pallas_call(
        functools.partial(_mm_resid_kernel, n_x=len(xs), n_a=xs[0].shape[0] // bm),
        grid=(nb, M // bm),
        in_specs=[pl.BlockSpec((bm, K), lambda j, i: (i, 0)),
                  pl.BlockSpec((K, bn), lambda j, i: (0, j))] + _row_specs(xs, bm, bn, 2) + [
                  pl.BlockSpec((1, 1, 1, bn), lambda j, i: (layer, row(i), 0, gate_blk * nb + j))],
        out_specs=pl.BlockSpec((bm, bn), lambda j, i: (i, j)),
        out_shape=jax.ShapeDtypeStruct((M, N), F32),
        compiler_params=_params(2, vmem_mb),
        name=name,
    )(a, w, *xs, mod4)


def _swiglu_kernel(a_ref, wg_ref, wu_ref, o_ref):
    wg = wg_ref[0].astype(BF16)
    wu = wu_ref[0].astype(BF16)
    for rows in _slabs(a_ref.shape[0]):
        a = a_ref[rows, :]
        g = _dot(a, wg)
        u = _dot(a, wu)
        o_ref[rows, :] = (g * jax.nn.sigmoid(g) * u).astype(o_ref.dtype)


def _swiglu_up(a, wg, wu, layer, bm, bn):
    M, K = a.shape
    N = wg.shape[2]
    bm = min(bm, M)
    return pl.pallas_call(
        _swiglu_kernel,
        grid=(M // bm, N // bn),
        in_specs=[pl.BlockSpec((bm, K), lambda i, j: (i, 0)),
                  pl.BlockSpec((1, K, bn), lambda i, j: (layer, 0, j)),
                  pl.BlockSpec((1, K, bn), lambda i, j: (layer, 0, j))],
        out_specs=pl.BlockSpec((bm, bn), lambda i, j: (i, j)),
        out_shape=jax.ShapeDtypeStruct((M, N), BF16),
        compiler_params=_params(2, 48),
        name="swiglu_up",
    )(a, wg, wu)


def _merge_kernel(af_ref, ap_ref, ao_ref, wf_ref, wp_ref, wo_ref, *refs, shift):
    o_ref = refs[-1]

    def gate(k, rows):
        main, nxt = refs[2 * k], refs[2 * k + 1]
        return jnp.concatenate([main[rows, shift:], nxt[rows, :shift]], axis=1).astype(F32)

    for rows in _slabs(af_ref.shape[0]):
        y = gate(0, rows) * _dot(af_ref[rows, :], wf_ref[...])
        y = y + gate(1, rows) * _dot(ap_ref[rows, :], wp_ref[...])
        y = y + gate(2, rows) * _dot(ao_ref[rows, :], wo_ref[...])
        o_ref[rows, :] = y.astype(o_ref.dtype)


def _merge(af, ap, ao, wf, wp, wo, gates, shift, bm, bn):
    M = af.shape[0]
    N = wf.shape[1]
    bm = min(bm, M)
    nb = N // bn
    a_spec = lambda a: pl.BlockSpec((bm, a.shape[1]), lambda j, i: (i, 0))
    w_spec = lambda w: pl.BlockSpec((w.shape[0], bn), lambda j, i: (0, j))
    g_specs = []
    for k in range(3):
        g_specs.append(pl.BlockSpec((bm, bn), lambda j, i, k=k: (i, k * nb + j)))
        g_specs.append(pl.BlockSpec((bm, LANE), lambda j, i, k=k: (i, (k * nb + j + 1) * (bn // LANE))))
    return pl.pallas_call(
        functools.partial(_merge_kernel, shift=shift),
        grid=(nb, M // bm),
        in_specs=[a_spec(af), a_spec(ap), a_spec(ao), w_spec(wf), w_spec(wp), w_spec(wo)] + g_specs,
        out_specs=pl.BlockSpec((bm, bn), lambda j, i: (i, j)),
        out_shape=jax.ShapeDtypeStruct((M, N), BF16),
        compiler_params=_params(2, 48),
        name="merge",
    )(af, ap, ao, wf, wp, wo, *([gates] * 6))


def _dft_kernel(f_ref, ch_ref, pc_ref, pl_ref, o_ref, *, n_ctx_chunks, s_ctx, fg):
    i = pl.program_id(0)
    tch = f_ref.shape[0]

    def mix(pos_ref, s):
        scale = 1.0 / float(np.sqrt(s * fg))
        for g in range(N_FOURIER_GROUPS):
            x = f_ref[:, g * fg:(g + 1) * fg].astype(BF16)
            xcs = _dot(x, ch_ref[...]).astype(BF16)
            for q in range(tch // s):
                rows = slice(q * s, (q + 1) * s)
                stacked = jnp.concatenate([xcs[rows, :fg], xcs[rows, fg:]], axis=0)
                y = _dot(pos_ref[...], stacked) * scale
                o_ref[rows, g * fg:(g + 1) * fg] = y.astype(o_ref.dtype)

    @pl.when(i < n_ctx_chunks)
    def _():
        mix(pc_ref, s_ctx)

    @pl.when(i >= n_ctx_chunks)
    def _():
        mix(pl_ref, tch)


def _dft_tables(s_ctx, s_lat, fg):
    def cs(n):
        k = np.arange(n, dtype=np.int64)
        ang = 2.0 * np.pi * ((k[:, None] * k[None, :]) % n).astype(np.float64) / n
        return np.cos(ang), np.sin(ang)
    cc, sc = cs(fg)
    chan = np.concatenate([cc, sc], axis=1)
    pos = []
    for s in (s_ctx, s_lat):
        c, sn = cs(s)
        pos.append(np.concatenate([c, -sn], axis=1))
    return (jnp.asarray(chan, BF16), jnp.asarray(pos[0], BF16), jnp.asarray(pos[1], BF16))


def _dft(z, dims, df):
    M = z.shape[0]
    tch, s_ctx = dims["tch"], dims["s_ctx"]
    fg = df // N_FOURIER_GROUPS
    chan, pos_c, pos_l = _dft_tables(s_ctx, tch, fg)
    full = lambda a: pl.BlockSpec(a.shape, lambda i: (0, 0))
    return pl.pallas_call(
        functools.partial(_dft_kernel, n_ctx_chunks=dims["m_ctx"] // tch, s_ctx=s_ctx, fg=fg),
        grid=(M // tch,),
        in_specs=[pl.BlockSpec((tch, df), lambda i: (i, 0)), full(chan), full(pos_c), full(pos_l)],
        out_specs=pl.BlockSpec((tch, df), lambda i: (i, 0)),
        out_shape=jax.ShapeDtypeStruct((M, df), BF16),
        compiler_params=_params(1, 48),
        name="dft_mix",
    )(z, chan, pos_c, pos_l)


def _pool_kernel(p_ref, ps_ref, o_ref, *, n_ctx_chunks, s_ctx, pg):
    i = pl.program_id(0)
    tch = p_ref.shape[0]
    s_seq = jnp.where(i < n_ctx_chunks, s_ctx, tch)
    pos = jax.lax.broadcasted_iota(jnp.int32, (tch, pg), 0) & (s_seq - 1)
    for gi, w in enumerate(POOL_WINDOWS):
        half = w // 2
        cols = slice(gi * pg, (gi + 1) * pg)
        x = p_ref[:, cols]
        tot = jnp.zeros_like(x)
        for d in range(-half, w - half):
            shifted = x if d == 0 else pltpu.roll(x, (-d) % tch, 0)
            ok = (pos + d >= 0) & (pos + d < s_seq)
            tot = tot + jnp.where(ok, shifted, 0.0)
        cnt = (jnp.minimum(pos + (w - half), s_seq) - jnp.maximum(pos - half, 0)).astype(F32)
        o_ref[:, cols] = ((tot / cnt - x) * ps_ref[:, cols]).astype(o_ref.dtype)


def _pool(z, pool_scale_l, dims, dp, col_blk):
    M = z.shape[0]
    tch = dims["tch"]
    return pl.pallas_call(
        functools.partial(_pool_kernel, n_ctx_chunks=dims["m_ctx"] // tch, s_ctx=dims["s_ctx"],
                          pg=dp // len(POOL_WINDOWS)),
        grid=(M // tch,),
        in_specs=[pl.BlockSpec((tch, dp), lambda i: (i, col_blk)),
                  pl.BlockSpec((1, dp), lambda i: (0, 0))],
        out_specs=pl.BlockSpec((tch, dp), lambda i: (i, 0)),
        out_shape=jax.ShapeDtypeStruct((M, dp), BF16),
        compiler_params=_params(1, 32),
        name="pool",
    )(z, pool_scale_l.reshape(1, dp))


def _rope_tables(n_tokens):
    rows = n_tokens // GRID_W
    row = np.repeat(np.arange(rows, dtype=np.float64), GRID_W)
    col = np.tile(np.arange(GRID_W, dtype=np.float64), rows)
    inv = ROPE_BASE ** (-np.arange(ROPE_DIM // 4, dtype=np.float64) * (4.0 / ROPE_DIM))
    ar = row[:, None] * inv[None, :]
    ac = col[:, None] * inv[None, :]
    ang = np.concatenate([ar, ar, ac, ac], axis=1)
    sign = np.tile(np.concatenate([-np.ones(ROPE_DIM // 4), np.ones(ROPE_DIM // 4)]), 2)
    cos = np.concatenate([np.cos(ang), np.ones_like(ang)], axis=1)
    sin = np.concatenate([np.sin(ang) * sign[None, :], np.zeros_like(ang)], axis=1)
    return jnp.asarray(cos, F32), jnp.asarray(sin, F32)


def _rope(x, cos, sin):
    q = ROPE_DIM // 4
    lane = jax.lax.broadcasted_iota(jnp.int32, x.shape, 1)
    first = (lane & (2 * q - 1)) < q
    swapped = jnp.where(first, pltpu.roll(x, LANE - q, 1), pltpu.roll(x, q, 1))
    return x * cos + swapped * sin


def _q_kernel(cq_ref, g_ref, w_ref, cos_ref, sin_ref, o_ref, *, n_ctx_tiles):
    i = pl.program_id(0)
    x = cq_ref[...]
    r = jax.lax.rsqrt(jnp.mean(x * x, axis=-1, keepdims=True) + EPS)
    q = _dot((x * r * g_ref[...]).astype(BF16), w_ref[...])

    @pl.when(i < n_ctx_tiles)
    def _():
        o_ref[...] = q.astype(o_ref.dtype)

    @pl.when(i >= n_ctx_tiles)
    def _():
        cos, sin = cos_ref[...], sin_ref[...]
        for h in range(N_HEADS):
            lo = h * HEAD_PAD
            o_ref[:, lo:lo + NOPE_DIM] = q[:, lo:lo + NOPE_DIM].astype(o_ref.dtype)
            pe = q[:, lo + NOPE_DIM:lo + HEAD_PAD]
            o_ref[:, lo + NOPE_DIM:lo + HEAD_PAD] = _rope(pe, cos, sin).astype(o_ref.dtype)


def _q_proj(z, g_q_l, w_q, cos, sin, dims, ql, col_blk):
    M = z.shape[0]
    N = w_q.shape[1]
    bm = min(256, dims["tch"])
    n_ctx_tiles = dims["m_ctx"] // bm
    per_seq = dims["tch"] // bm
    pos_blk = lambda i: (jnp.maximum(i - n_ctx_tiles, 0) % per_seq, 0)
    return pl.pallas_call(
        functools.partial(_q_kernel, n_ctx_tiles=n_ctx_tiles),
        grid=(M // bm,),
        in_specs=[pl.BlockSpec((bm, ql), lambda i: (i, col_blk)),
                  pl.BlockSpec((1, ql), lambda i: (0, 0)),
                  pl.BlockSpec((ql, N), lambda i: (0, 0)),
                  pl.BlockSpec((bm, LANE), pos_blk),
                  pl.BlockSpec((bm, LANE), pos_blk)],
        out_specs=pl.BlockSpec((bm, N), lambda i: (i, 0)),
        out_shape=jax.ShapeDtypeStruct((M, N), BF16),
        compiler_params=_params(1, 48),
        name="q_proj",
    )(z, g_q_l.reshape(1, ql), w_q, cos, sin)


def _kv_kernel(ckv_ref, g_ref, w_ref, kpe_ref, cos_ref, sin_ref, ckv_o, kv_o, kpe_o, *, n_ctx_tiles):
    i = pl.program_id(0)
    x = ckv_ref[...]
    r = jax.lax.rsqrt(jnp.mean(x * x, axis=-1, keepdims=True) + EPS)
    c = x * r * g_ref[...]
    ckv_o[...] = c
    kv_o[...] = _dot(c.astype(BF16), w_ref[...]).astype(kv_o.dtype)
    kpe = kpe_ref[...]
    kpe = jnp.where(jax.lax.broadcasted_iota(jnp.int32, kpe.shape, 1) < ROPE_DIM, kpe, 0.0)

    @pl.when(i < n_ctx_tiles)
    def _():
        kpe_o[...] = kpe.astype(kpe_o.dtype)

    @pl.when(i >= n_ctx_tiles)
    def _():
        kpe_o[...] = _rope(kpe, cos_ref[...], sin_ref[...]).astype(kpe_o.dtype)


def _kv_proj(z, kpe_raw, g_kv_l, w_kv, cos, sin, dims, kvl, ckv_blk):
    M = z.shape[0]
    N = w_kv.shape[1]
    bm = min(256, dims["tch"])
    n_ctx_tiles = dims["m_ctx"] // bm
    per_seq = dims["tch"] // bm
    pos_blk = lambda i: (jnp.maximum(i - n_ctx_tiles, 0) % per_seq, 0)
    return pl.pallas_call(
        functools.partial(_kv_kernel, n_ctx_tiles=n_ctx_tiles),
        grid=(M // bm,),
        in_specs=[pl.BlockSpec((bm, kvl), lambda i: (i, ckv_blk)),
                  pl.BlockSpec((1, kvl), lambda i: (0, 0)),
                  pl.BlockSpec((kvl, N), lambda i: (0, 0)),
                  pl.BlockSpec((bm, LANE), lambda i: (i, 0)),
                  pl.BlockSpec((bm, LANE), pos_blk),
                  pl.BlockSpec((bm, LANE), pos_blk)],
        out_specs=[pl.BlockSpec((bm, kvl), lambda i: (i, 0)),
                   pl.BlockSpec((bm, N), lambda i: (i, 0)),
                   pl.BlockSpec((bm, LANE), lambda i: (i, 0))],
        out_shape=[jax.ShapeDtypeStruct((M, kvl), F32),
                   jax.ShapeDtypeStruct((M, N), BF16),
                   jax.ShapeDtypeStruct((M, LANE), BF16)],
        compiler_params=_params(1, 32),
        name="kv_proj",
    )(z, g_kv_l.reshape(1, kvl), w_kv, kpe_raw, cos, sin)


def _kv_cache_kernel(c_ref, w_ref, o_ref):
    o_ref[...] = _dot(c_ref[0, 0].astype(BF16), w_ref[...]).astype(o_ref.dtype)


def _kv_cache_proj(cache_ckv, w_kv, layer):
    DB, _, P, kvl = cache_ckv.shape
    N = w_kv.shape[1]
    return pl.pallas_call(
        _kv_cache_kernel,
        grid=(DB,),
        in_specs=[pl.BlockSpec((1, 1, P, kvl), lambda b: (b, layer, 0, 0)),
                  pl.BlockSpec((kvl, N), lambda b: (0, 0))],
        out_specs=pl.BlockSpec((P, N), lambda b: (b, 0)),
        out_shape=jax.ShapeDtypeStruct((DB * P, N), BF16),
        compiler_params=_params(1, 32),
        name="kv_cache_proj",
    )(cache_ckv, w_kv)


def _softmax_rows(parts):
    m = functools.reduce(jnp.maximum, [jnp.max(s, axis=-1, keepdims=True) for s in parts])
    es = [jnp.exp(s - m) for s in parts]
    tot = functools.reduce(jnp.add, [jnp.sum(e, axis=-1, keepdims=True) for e in es])
    inv = 1.0 / tot
    return [(e * inv).astype(BF16) for e in es]


def _attn_ctx_kernel(q_ref, kv_ref, kpe_ref, o_ref):
    kpe = kpe_ref[...]
    for h in range(N_HEADS):
        lo = h * HEAD_PAD
        k = jnp.concatenate([kv_ref[:, lo:lo + NOPE_DIM], kpe], axis=1)
        s = _dot_nt(q_ref[:, lo:lo + HEAD_PAD], k) * SM_SCALE
        (p,) = _softmax_rows([s])
        o_ref[:, h * V_DIM:(h + 1) * V_DIM] = _dot(p, kv_ref[:, lo + NOPE_DIM:lo + HEAD_PAD]).astype(o_ref.dtype)


def _attn_ctx(q, kv, kpe, dims):
    M = q.shape[0]
    s = dims["s_ctx"]
    n = dims["m_ctx"] // s
    dv = N_HEADS * V_DIM
    return pl.pallas_call(
        _attn_ctx_kernel,
        grid=(n,),
        in_specs=[pl.BlockSpec((s, q.shape[1]), lambda b: (b, 0)),
                  pl.BlockSpec((s, kv.shape[1]), lambda b: (b, 0)),
                  pl.BlockSpec((s, LANE), lambda b: (b, 0))],
        out_specs=pl.BlockSpec((s, dv), lambda b: (b, 0)),
        out_shape=jax.ShapeDtypeStruct((M, dv), BF16),
        compiler_params=_params(1, 32),
        name="attn_ctx",
    )(q, kv, kpe)


def _attn_lat_kernel(o_in_ref, q_ref, kv_ref, kpe_ref, kvc_ref, kpec_ref, o_ref):
    del o_in_ref
    kpe = kpe_ref[...]
    kpec = kpec_ref[0, 0].astype(BF16)
    kpec = jnp.concatenate([kpec, jnp.zeros_like(kpec)], axis=1)
    for h in range(N_HEADS):
        lo = h * HEAD_PAD
        qh = q_ref[:, lo:lo + HEAD_PAD]
        k_own = jnp.concatenate([kv_ref[:, lo:lo + NOPE_DIM], kpe], axis=1)
        k_c = jnp.concatenate([kvc_ref[:, lo:lo + NOPE_DIM], kpec], axis=1)
        s_c = _dot_nt(qh, k_c) * SM_SCALE
        s_o = _dot_nt(qh, k_own) * SM_SCALE
        p_c, p_o = _softmax_rows([s_c, s_o])
        o = _dot(p_c, kvc_ref[:, lo + NOPE_DIM:lo + HEAD_PAD]) + _dot(p_o, kv_ref[:, lo + NOPE_DIM:lo + HEAD_PAD])
        o_ref[:, h * V_DIM:(h + 1) * V_DIM] = o.astype(o_ref.dtype)


def _attn_lat(o_buf, q, kv, kpe, kv_cache, cache_krope, layer, dims):
    tch, m_ctx = dims["tch"], dims["m_ctx"]
    DB, _, P, rd = cache_krope.shape
    bq = min(256, tch)
    nq = tch // bq
    dv = N_HEADS * V_DIM
    q_blk = lambda b, t: (m_ctx // bq + b * nq + t, 0)
    own_blk = lambda b, t: (m_ctx // tch + b, 0)
    return pl.pallas_call(
        _attn_lat_kernel,
        grid=(DB, nq),
        in_specs=[pl.BlockSpec(memory_space=pl.ANY),
                  pl.BlockSpec((bq, q.shape[1]), q_blk),
                  pl.BlockSpec((tch, kv.shape[1]), own_blk),
                  pl.BlockSpec((tch, LANE), own_blk),
                  pl.BlockSpec((P, kv_cache.shape[1]), lambda b, t: (b, 0)),
                  pl.BlockSpec((1, 1, P, rd), lambda b, t: (b, layer, 0, 0))],
        out_specs=pl.BlockSpec((bq, dv), q_blk),
        out_shape=jax.ShapeDtypeStruct(o_buf.shape, o_buf.dtype),
        input_output_aliases={0: 0},
        compiler_params=_params(2, 48),
        name="attn_lat",
    )(o_buf, q, kv, kpe, kv_cache, cache_krope)


def kernel(x_prompt, x_sample, cache_ckv, cache_krope, c, c_ctx, w_ada, b_ada, g_norm1, g_norm2, w_in, w_fourier,
           pool_scale, w_pool, g_q, w_q_b, g_kv, w_kv_b, w_o_mla, w_out, w_gate, w_up, w_down, g_final):
    B, S, D = x_prompt.shape
    DB, DS, _ = x_sample.shape
    L = w_ada.shape[0]
    DF = w_fourier.shape[1]
    DP = pool_scale.shape[1]
    QL = g_q.shape[1]
    KVL = g_kv.shape[1]
    m_ctx, m_lat = B * S, DB * DS
    M = m_ctx + m_lat
    assert DS % S == 0 and m_ctx % DS == 0 and S & (S - 1) == 0 and DS & (DS - 1) == 0
    assert 1 + DB <= MOD_ROWS and DS % GRID_W == 0
    dims = dict(m_ctx=m_ctx, tch=DS, s_ctx=S)

    off_p, off_q, off_kv, off_pe = DF, DF + DP, DF + DP + QL, DF + DP + QL + KVL
    assert off_p % DP == 0 and off_q % QL == 0 and off_kv % KVL == 0 and off_pe % IN_BN == 0

    xs = (x_prompt.reshape(m_ctx, D), x_sample.reshape(m_lat, D))
    cond = jnp.zeros((MOD_ROWS, D), F32).at[0].set(c_ctx).at[1:1 + DB].set(c)
    mod4 = _ada_mod(cond, w_ada, b_ada).reshape(L, MOD_ROWS, 1, 6 * D)
    cos, sin = _rope_tables(DS)

    ckv_layers, kpe_layers = [], []
    for l in range(L):
        w_q =jnp.pad(w_q_b[l].reshape(QL, N_HEADS, NOPE_DIM + ROPE_DIM),
                      ((0, 0), (0, 0), (0, HEAD_PAD - NOPE_DIM - ROPE_DIM))).reshape(QL, N_HEADS * HEAD_PAD).astype(BF16)
        w_kv = w_kv_b[l].astype(BF16)

        h = _norm_mod(xs, g_norm1[l:l + 1], mod4, l, 0, dims)
        z, kpe_raw, gates = _in_proj(h, w_in, l, off_pe, 1024)

        mixed = _dft(z, dims, DF)
        pooled = _pool(z, pool_scale[l], dims, DP, off_p // DP)
        q = _q_proj(z, g_q[l], w_q, cos, sin, dims, QL, off_q // QL)
        c_kv, kv, kpe = _kv_proj(z, kpe_raw, g_kv[l], w_kv, cos, sin, dims, KVL, off_kv // KVL)
        kv_cache = _kv_cache_proj(cache_ckv, w_kv, l)
        o = _attn_ctx(q, kv, kpe, dims)
        o = _attn_lat(o, q, kv, kpe, kv_cache, cache_krope, l, dims)

        merged = _merge(mixed, pooled, o, w_fourier[l].astype(BF16), w_pool[l].reshape(DP, D).astype(BF16),
                        w_o_mla[l].astype(BF16), gates, ROPE_DIM, 512, 1024)
        x = _mm_resid(merged, w_out[l].astype(BF16), xs, mod4, l, 2, 512, 1024, dims, 48, "out_proj")

        h = _norm_mod((x,), g_norm2[l:l + 1], mod4, l, 3, dims)
        act = _swiglu_up(h, w_gate, w_up, l, 1024, 256)
        x = _mm_resid(act, w_down[l].astype(BF16), (x,), mod4, l, 5, 512, 512, dims, 56, "down_proj")
        xs = (x,)

        ckv_layers.append(c_kv[:m_ctx].reshape(B, S, KVL))
        kpe_layers.append(kpe_raw[:m_ctx, :ROPE_DIM].reshape(B, S, ROPE_DIM))

    y_prompt = _final_norm(x, g_final, 0, m_ctx).reshape(B, S, D)
    y_sample = _final_norm(x, g_final, m_ctx, m_lat).reshape(DB, DS, D)
    return (y_prompt, y_sample, jnp.stack(ckv_layers, axis=1), jnp.stack(kpe_layers, axis=1))
```

```python
import functools

import numpy as np
import jax
import jax.numpy as jnp
from jax.experimental import pallas as pl
from jax.experimental.pallas import tpu as pltpu

N_HEADS = 16
NOPE_DIM = 128
ROPE_DIM = 64
V_DIM = 128
HEAD_PAD = 256
GRID_W = 64
EPS = 1e-6
ROPE_BASE = 10000.0
SM_SCALE = (NOPE_DIM + ROPE_DIM) ** -0.5
POOL_WINDOWS = (2, 4, 8, 16)
N_FOURIER_GROUPS = 4
MOD_ROWS = 8
LANE = 128

BF16 = jnp.bfloat16
F32 = jnp.float32


def _params(n_grid, vmem_mb):
    return pltpu.CompilerParams(dimension_semantics=("arbitrary",) * n_grid,
                                vmem_limit_bytes=vmem_mb * 2 ** 20)


def _dot(a, b):
    return jnp.dot(a, b, preferred_element_type=F32)


def _dot_nt(a, b):
    return jax.lax.dot_general(a, b, (((1,), (1,)), ((), ())), preferred_element_type=F32)


SLAB_ROWS = 256


def _slabs(n_rows):
    r = int(np.gcd(SLAB_ROWS, n_rows))
    return [slice(s * r, (s + 1) * r) for s in range(n_rows // r)]


def _mod_row(i, bm, m_ctx, tch):
    start = i * bm
    return jnp.where(start < m_ctx, 0, 1 + (start - m_ctx) // tch)


def _ada_kernel(c_ref, w_ref, b_ref, o_ref):
    c = c_ref[...]
    a = (c * jax.nn.sigmoid(c)).astype(BF16)
    o_ref[0] = _dot(a, w_ref[0].astype(BF16)) + b_ref[0]


def _ada_mod(cond, w_ada, b_ada):
    L, D, N = w_ada.shape
    bn = 512
    return pl.pallas_call(
        _ada_kernel,
        grid=(L, N // bn),
        in_specs=[pl.BlockSpec((MOD_ROWS, D), lambda l, j: (0, 0)),
                  pl.BlockSpec((1, D, bn), lambda l, j: (l, 0, j)),
                  pl.BlockSpec((1, 1, bn), lambda l, j: (l, 0, j))],
        out_specs=pl.BlockSpec((1, MOD_ROWS, bn), lambda l, j: (l, 0, j)),
        out_shape=jax.ShapeDtypeStruct((L, MOD_ROWS, N), F32),
        compiler_params=_params(2, 40),
        name="ada_mod",
    )(cond, w_ada, b_ada.reshape(L, 1, N))


def _row_specs(xs, bm, bn, n_grid):
    idx = (lambda *g: (g[-1], g[0])) if n_grid == 2 else (lambda *g: (g[-1], 0))
    if len(xs) == 1:
        return [pl.BlockSpec((bm, bn), idx)]
    n_a = xs[0].shape[0] // bm
    last_b = xs[1].shape[0] // bm - 1
    spec_a = pl.BlockSpec((bm, bn), lambda *g: (jnp.minimum(idx(*g)[0], n_a - 1), idx(*g)[1]))
    spec_b = pl.BlockSpec((bm, bn), lambda *g: (jnp.clip(idx(*g)[0] - n_a, 0, last_b), idx(*g)[1]))
    return [spec_a, spec_b]


def _norm_mod_kernel(*refs, n_x, n_a):
    x_refs, (g_ref, sh_ref, sc_ref, o_ref) = refs[:n_x], refs[n_x:]
    x = x_refs[0][...] if n_x == 1 else jnp.where(pl.program_id(0) < n_a, x_refs[0][...], x_refs[1][...])
    r = jax.lax.rsqrt(jnp.mean(x * x, axis=-1, keepdims=True) + EPS)
    y = x * r * g_ref[...]
    o_ref[...] = (y * (1.0 + sc_ref[0, 0]) + sh_ref[0, 0]).astype(o_ref.dtype)


def _norm_mod(xs, g, mod4, layer, shift_blk, dims):
    M = sum(x.shape[0] for x in xs)
    D = xs[0].shape[1]
    bm = min(256, dims["tch"])
    row = functools.partial(_mod_row, bm=bm, m_ctx=dims["m_ctx"], tch=dims["tch"])
    return pl.pallas_call(
        functools.partial(_norm_mod_kernel, n_x=len(xs), n_a=xs[0].shape[0] // bm),
        grid=(M // bm,),
        in_specs=_row_specs(xs, bm, D, 1) + [
            pl.BlockSpec((1, D), lambda i: (0, 0)),
            pl.BlockSpec((1, 1, 1, D), lambda i: (layer, row(i), 0, shift_blk)),
            pl.BlockSpec((1, 1, 1, D), lambda i: (layer, row(i), 0, shift_blk + 1))],
        out_specs=pl.BlockSpec((bm, D), lambda i: (i, 0)),
        out_shape=jax.ShapeDtypeStruct((M, D), BF16),
        compiler_params=_params(1, 32),
        name="norm_mod",
    )(*xs, g, mod4, mod4)


def _final_norm_kernel(x_ref, g_ref, o_ref):
    x = x_ref[...]
    r = jax.lax.rsqrt(jnp.mean(x * x, axis=-1, keepdims=True) + EPS)
    o_ref[...] = x * r * g_ref[...]


def _final_norm(x, g, row0, rows):
    D = x.shape[1]
    bm = int(np.gcd(256, np.gcd(rows, row0 if row0 else rows)))
    blk0 = row0 // bm
    return pl.pallas_call(
        _final_norm_kernel,
        grid=(rows // bm,),
        in_specs=[pl.BlockSpec((bm, D), lambda i: (blk0 + i, 0)),
                  pl.BlockSpec((1, D), lambda i: (0, 0))],
        out_specs=pl.BlockSpec((bm, D), lambda i: (i, 0)),
        out_shape=jax.ShapeDtypeStruct((rows, D), F32),
        compiler_params=_params(1, 32),
        name="final_norm",
    )(x, g.reshape(1, D))


IN_BN = 512


def _in_proj_kernel(h_ref, w_ref, wk_ref, z_ref, kpe_ref, g_ref, *, n_small_blk):
    j = pl.program_id(1)

    @pl.when(j < n_small_blk)
    def _():
        w = w_ref[0].astype(BF16)
        for rows in _slabs(h_ref.shape[0]):
            z_ref[rows, :] = _dot_nt(h_ref[rows, :], w)

    @pl.when(j == n_small_blk)
    def _():
        kpe_ref[...] = _dot_nt(h_ref[...], wk_ref[0].astype(BF16))

    @pl.when(j >= n_small_blk)
    def _():
        w = w_ref[0].astype(BF16)
        for rows in _slabs(h_ref.shape[0]):
            g_ref[rows, :] = jax.nn.sigmoid(_dot_nt(h_ref[rows, :], w)).astype(g_ref.dtype)


def _in_proj(h, w_t, layer, off_pe, n_small, bm):
    M, K = h.shape
    N = w_t.shape[1]
    bm = min(bm, M)
    n_small_blk = off_pe // IN_BN
    n_gate_blk = (N - n_small) // IN_BN
    assert off_pe % IN_BN == 0 and (N - n_small) % IN_BN == 0 and n_small % 8 == 0 and off_pe % LANE == 0

    def w_row(j):
        row = jnp.where(j < n_small_blk, j * IN_BN, n_small + (j - n_small_blk) * IN_BN)
        return pl.multiple_of(row, 8)

    return pl.pallas_call(
        functools.partial(_in_proj_kernel, n_small_blk=n_small_blk),
        grid=(M // bm, n_small_blk + n_gate_blk),
        in_specs=[pl.BlockSpec((bm, K), lambda i, j: (i, 0)),
                  pl.BlockSpec((pl.Element(1), pl.Element(IN_BN), pl.Element(K)), lambda i, j: (layer, w_row(j), 0)),
                  pl.BlockSpec((1, LANE, K), lambda i, j: (layer, off_pe // LANE, 0))],
        out_specs=[pl.BlockSpec((bm, IN_BN), lambda i, j: (i, jnp.minimum(j, n_small_blk - 1))),
                   pl.BlockSpec((bm, LANE), lambda i, j: (i, 0)),
                   pl.BlockSpec((bm, IN_BN), lambda i, j: (i, jnp.maximum(j - n_small_blk, 0)))],
        out_shape=[jax.ShapeDtypeStruct((M, off_pe), F32),
                   jax.ShapeDtypeStruct((M, LANE), F32),
                   jax.ShapeDtypeStruct((M, n_gate_blk * IN_BN), BF16)],
        compiler_params=_params(2, 56),
        name="in_proj",
    )(h, w_t, w_t)


def _mm_resid_kernel(a_ref, w_ref, *refs, n_x, n_a):
    x_refs, (gate_ref, o_ref) = refs[:n_x], refs[n_x:]
    first = pl.program_id(1) < n_a
    for rows in _slabs(a_ref.shape[0]):
        x = x_refs[0][rows, :] if n_x == 1 else jnp.where(first, x_refs[0][rows, :], x_refs[1][rows, :])
        o_ref[rows, :] = x + gate_ref[0, 0] * _dot(a_ref[rows, :], w_ref[0])


def _mm_resid(a, w, xs, mod4, layer, gate_blk, bm, bn, dims, vmem_mb, name):
    M, K = a.shape
    N = w.shape[2]
    bm = min(bm, dims["tch"])
    row = functools.partial(_mod_row, bm=bm, m_ctx=dims["m_ctx"], tch=dims["tch"])
    nb = N // bn
    return pl.pallas_call(
        functools.partial(_mm_resid_kernel, n_x=len(xs), n_a=xs[0].shape[0] // bm),
        grid=(nb, M // bm),
        in_specs=[pl.BlockSpec((bm, K), lambda j, i: (i, 0)),
                  pl.BlockSpec((1, K, bn), lambda j, i: (layer, 0, j))] + _row_specs(xs, bm, bn, 2) + [
                  pl.BlockSpec((1, 1, 1, bn), lambda j, i: (layer, row(i), 0, gate_blk * nb + j))],
        out_specs=pl.BlockSpec((bm, bn), lambda j, i: (i, j)),
        out_shape=jax.ShapeDtypeStruct((M, N), F32),
        compiler_params=_params(2, vmem_mb),
        name=name,
    )(a, w, *xs, mod4)


def _swiglu_kernel(a_ref, wg_ref, wu_ref, o_ref):
    wg = wg_ref[0].astype(BF16)
    wu = wu_ref[0].astype(BF16)
    for rows in _slabs(a_ref.shape[0]):
        a = a_ref[rows, :]
        g = _dot(a, wg)
        u = _dot(a, wu)
        o_ref[rows, :] = (g * jax.nn.sigmoid(g) * u).astype(o_ref.dtype)


def _swiglu_up(a, wg, wu, layer, bm, bn):
    M, K = a.shape
    N = wg.shape[2]
    bm = min(bm, M)
    return pl.pallas_call(
        _swiglu_kernel,
        grid=(M // bm, N // bn),
        in_specs=[pl.BlockSpec((bm, K), lambda i, j: (i, 0)),
                  pl.BlockSpec((1, K, bn), lambda i, j: (layer, 0, j)),
                  pl.BlockSpec((1, K, bn), lambda i, j: (layer, 0, j))],
        out_specs=pl.BlockSpec((bm, bn), lambda i, j: (i, j)),
        out_shape=jax.ShapeDtypeStruct((M, N), BF16),
        compiler_params=_params(2, 48),
        name="swiglu_up",
    )(a, wg, wu)


def _merge_kernel(af_ref, ap_ref, aoc_ref, aol_ref, wf_ref, wp_ref, wo_ref, g0_ref, g1_ref, g2_ref, o_ref, *, n_a):
    first = pl.program_id(1) < n_a
    for rows in _slabs(af_ref.shape[0]):
        ao = jnp.where(first, aoc_ref[rows, :], aol_ref[rows, :])
        y = g0_ref[rows, :].astype(F32) * _dot(af_ref[rows, :], wf_ref[0])
        y = y + g1_ref[rows, :].astype(F32) * _dot(ap_ref[rows, :], wp_ref[0])
        y = y + g2_ref[rows, :].astype(F32) * _dot(ao, wo_ref[0])
        o_ref[rows, :] = y.astype(o_ref.dtype)


def _merge(af, ap, aos, wf, wp, wo, gates, layer, bm, bn):
    M = af.shape[0]
    N = wf.shape[2]
    bm = min(bm, aos[0].shape[0], aos[1].shape[0])
    nb = N // bn
    a_spec = lambda a: pl.BlockSpec((bm, a.shape[1]), lambda j, i: (i, 0))
    w_spec = lambda w: pl.BlockSpec((1, w.shape[1], bn), lambda j, i: (layer, 0, j))
    g_spec = lambda k: pl.BlockSpec((bm, bn), lambda j, i: (i, k * nb + j))
    return pl.pallas_call(
        functools.partial(_merge_kernel, n_a=aos[0].shape[0] // bm),
        grid=(nb, M // bm),
        in_specs=[a_spec(af), a_spec(ap)] + _row_specs(aos, bm, aos[0].shape[1], 1) + [
                  w_spec(wf), w_spec(wp), w_spec(wo), g_spec(0), g_spec(1), g_spec(2)],
        out_specs=pl.BlockSpec((bm, bn), lambda j, i: (i, j)),
        out_shape=jax.ShapeDtypeStruct((M, N), BF16),
        compiler_params=_params(2, 48),
        name="merge",
    )(af, ap, *aos, wf, wp, wo, gates, gates, gates)


def _dft_kernel(f_ref, ch_ref, pc_ref, pl_ref, o_ref, *, n_ctx_chunks, s_ctx, fg):
    i = pl.program_id(0)
    tch = f_ref.shape[0]

    def mix(pos_ref, s):
        scale = 1.0 / float(np.sqrt(s * fg))
        for g in range(N_FOURIER_GROUPS):
            x = f_ref[:, g * fg:(g + 1) * fg].astype(BF16)
            xcs = _dot(x, ch_ref[...]).astype(BF16)
            for q in range(tch // s):
                rows = slice(q * s, (q + 1) * s)
                stacked = jnp.concatenate([xcs[rows, :fg], xcs[rows, fg:]], axis=0)
                y = _dot(pos_ref[...], stacked) * scale
                o_ref[rows, g * fg:(g + 1) * fg] = y.astype(o_ref.dtype)

    @pl.when(i < n_ctx_chunks)
    def _():
        mix(pc_ref, s_ctx)

    @pl.when(i >= n_ctx_chunks)
    def _():
        mix(pl_ref, tch)


def _dft_tables(s_ctx, s_lat, fg):
    def cs(n):
        k = np.arange(n, dtype=np.int64)
        ang = 2.0 * np.pi * ((k[:, None] * k[None, :]) % n).astype(np.float64) / n
        return np.cos(ang), np.sin(ang)
    cc, sc = cs(fg)
    chan = np.concatenate([cc, sc], axis=1)
    pos = []
    for s in (s_ctx, s_lat):
        c, sn = cs(s)
        pos.append(np.concatenate([c, -sn], axis=1))
    return (jnp.asarray(chan, BF16), jnp.asarray(pos[0], BF16), jnp.asarray(pos[1], BF16))


def _dft(z, dims, df):
    M = z.shape[0]
    tch, s_ctx = dims["tch"], dims["s_ctx"]
    fg = df // N_FOURIER_GROUPS
    chan, pos_c, pos_l = _dft_tables(s_ctx, tch, fg)
    full = lambda a: pl.BlockSpec(a.shape, lambda i: (0, 0))
    return pl.pallas_call(
        functools.partial(_dft_kernel, n_ctx_chunks=dims["m_ctx"] // tch, s_ctx=s_ctx, fg=fg),
        grid=(M // tch,),
        in_specs=[pl.BlockSpec((tch, df), lambda i: (i, 0)), full(chan), full(pos_c), full(pos_l)],
        out_specs=pl.BlockSpec((tch, df), lambda i: (i, 0)),
        out_shape=jax.ShapeDtypeStruct((M, df), BF16),
        compiler_params=_params(1, 48),
        name="dft_mix",
    )(z, chan, pos_c, pos_l)


def _pool_kernel(p_ref, ps_ref, o_ref, *, n_ctx_chunks, s_ctx, pg):
    i = pl.program_id(0)
    tch = p_ref.shape[0]
    s_seq = jnp.where(i < n_ctx_chunks, s_ctx, tch)
    pos = jax.lax.broadcasted_iota(jnp.int32, (tch, pg), 0) & (s_seq - 1)
    for gi, w in enumerate(POOL_WINDOWS):
        half = w // 2
        cols = slice(gi * pg, (gi + 1) * pg)
        x = p_ref[:, cols]
        tot = jnp.zeros_like(x)
        for d in range(-half, w - half):
            shifted = x if d == 0 else pltpu.roll(x, (-d) % tch, 0)
            ok = (pos + d >= 0) & (pos + d < s_seq)
            tot = tot + jnp.where(ok, shifted, 0.0)
        cnt = (jnp.minimum(pos + (w - half), s_seq) - jnp.maximum(pos - half, 0)).astype(F32)
        o_ref[:, cols] = ((tot / cnt - x) * ps_ref[:, cols]).astype(o_ref.dtype)


def _pool(z, pool_scale_l, dims, dp, col_blk):
    M = z.shape[0]
    tch = dims["tch"]
    return pl.pallas_call(
        functools.partial(_pool_kernel, n_ctx_chunks=dims["m_ctx"] // tch, s_ctx=dims["s_ctx"],
                          pg=dp // len(POOL_WINDOWS)),
        grid=(M // tch,),
        in_specs=[pl.BlockSpec((tch, dp), lambda i: (i, col_blk)),
                  pl.BlockSpec((1, dp), lambda i: (0, 0))],
        out_specs=pl.BlockSpec((tch, dp), lambda i: (i, 0)),
        out_shape=jax.ShapeDtypeStruct((M, dp), BF16),
        compiler_params=_params(1, 32),
        name="pool",
    )(z, pool_scale_l.reshape(1, dp))


def _rope_tables(n_tokens):
    rows = n_tokens // GRID_W
    row = np.repeat(np.arange(rows, dtype=np.float64), GRID_W)
    col = np.tile(np.arange(GRID_W, dtype=np.float64), rows)
    inv = ROPE_BASE ** (-np.arange(ROPE_DIM // 4, dtype=np.float64) * (4.0 / ROPE_DIM))
    ar = row[:, None] * inv[None, :]
    ac = col[:, None] * inv[None, :]
    ang = np.concatenate([ar, ar, ac, ac], axis=1)
    sign = np.tile(np.concatenate([-np.ones(ROPE_DIM // 4), np.ones(ROPE_DIM // 4)]), 2)
    cos = np.concatenate([np.cos(ang), np.ones_like(ang)], axis=1)
    sin = np.concatenate([np.sin(ang) * sign[None, :], np.zeros_like(ang)], axis=1)
    return jnp.asarray(cos, F32), jnp.asarray(sin, F32)


def _rope(x, cos, sin):
    q = ROPE_DIM // 4
    lane = jax.lax.broadcasted_iota(jnp.int32, x.shape, 1)
    first = (lane & (2 * q - 1)) < q
    swapped = jnp.where(first, pltpu.roll(x, LANE - q, 1), pltpu.roll(x, q, 1))
    return x * cos + swapped * sin


def _q_kernel(cq_ref, g_ref, w_ref, cos_ref, sin_ref, o_ref, *, n_ctx_tiles):
    i = pl.program_id(0)
    x = cq_ref[...]
    r = jax.lax.rsqrt(jnp.mean(x * x, axis=-1, keepdims=True) + EPS)
    q = _dot((x * r * g_ref[...]).astype(BF16), w_ref[...])

    @pl.when(i < n_ctx_tiles)
    def _():
        o_ref[...] = q.astype(o_ref.dtype)

    @pl.when(i >= n_ctx_tiles)
    def _():
        cos, sin = cos_ref[...], sin_ref[...]
        for h in range(N_HEADS):
            lo = h * HEAD_PAD
            o_ref[:, lo:lo + NOPE_DIM] = q[:, lo:lo + NOPE_DIM].astype(o_ref.dtype)
            pe = q[:, lo + NOPE_DIM:lo + HEAD_PAD]
            o_ref[:, lo + NOPE_DIM:lo + HEAD_PAD] = _rope(pe, cos, sin).astype(o_ref.dtype)


def _q_proj(z, g_q_l, w_q, cos, sin, dims, ql, col_blk):
    M = z.shape[0]
    N = w_q.shape[1]
    bm = min(256, dims["tch"])
    n_ctx_tiles = dims["m_ctx"] // bm
    per_seq = dims["tch"] // bm
    pos_blk = lambda i: (jnp.maximum(i - n_ctx_tiles, 0) % per_seq, 0)
    return pl.pallas_call(
        functools.partial(_q_kernel, n_ctx_tiles=n_ctx_tiles),
        grid=(M // bm,),
        in_specs=[pl.BlockSpec((bm, ql), lambda i: (i, col_blk)),
                  pl.BlockSpec((1, ql), lambda i: (0, 0)),
                  pl.BlockSpec((ql, N), lambda i: (0, 0)),
                  pl.BlockSpec((bm, LANE), pos_blk),
                  pl.BlockSpec((bm, LANE), pos_blk)],
        out_specs=pl.BlockSpec((bm, N), lambda i: (i, 0)),
        out_shape=jax.ShapeDtypeStruct((M, N), BF16),
        compiler_params=_params(1, 48),
        name="q_proj",
    )(z, g_q_l.reshape(1, ql), w_q, cos, sin)


def _kv_kernel(ckv_ref, g_ref, w_ref, kpe_ref, cos_ref, sin_ref, ckv_o, kv_o, kpe_o, *, n_ctx_tiles):
    i = pl.program_id(0)
    x = ckv_ref[...]
    r = jax.lax.rsqrt(jnp.mean(x * x, axis=-1, keepdims=True) + EPS)
    c = x * r * g_ref[...]
    ckv_o[...] = c
    kv_o[...] = _dot(c.astype(BF16), w_ref[...]).astype(kv_o.dtype)
    kpe = kpe_ref[...]
    kpe = jnp.where(jax.lax.broadcasted_iota(jnp.int32, kpe.shape, 1) < ROPE_DIM, kpe, 0.0)

    @pl.when(i < n_ctx_tiles)
    def _():
        kpe_o[...] = kpe.astype(kpe_o.dtype)

    @pl.when(i >= n_ctx_tiles)
    def _():
        kpe_o[...] = _rope(kpe, cos_ref[...], sin_ref[...]).astype(kpe_o.dtype)


def _kv_proj(z, kpe_raw, g_kv_l, w_kv, cos, sin, dims, kvl, ckv_blk):
    M = z.shape[0]
    N = w_kv.shape[1]
    bm = min(256, dims["tch"])
    n_ctx_tiles = dims["m_ctx"] // bm
    per_seq = dims["tch"] // bm
    pos_blk = lambda i: (jnp.maximum(i - n_ctx_tiles, 0) % per_seq, 0)
    return pl.pallas_call(
        functools.partial(_kv_kernel, n_ctx_tiles=n_ctx_tiles),
        grid=(M // bm,),
        in_specs=[pl.BlockSpec((bm, kvl), lambda i: (i, ckv_blk)),
                  pl.BlockSpec((1, kvl), lambda i: (0, 0)),
                  pl.BlockSpec((kvl, N), lambda i: (0, 0)),
                  pl.BlockSpec((bm, LANE), lambda i: (i, 0)),
                  pl.BlockSpec((bm, LANE), pos_blk),
                  pl.BlockSpec((bm, LANE), pos_blk)],
        out_specs=[pl.BlockSpec((bm, kvl), lambda i: (i, 0)),
                   pl.BlockSpec((bm, N), lambda i: (i, 0)),
                   pl.BlockSpec((bm, LANE), lambda i: (i, 0))],
        out_shape=[jax.ShapeDtypeStruct((M, kvl), F32),
                   jax.ShapeDtypeStruct((M, N), BF16),
                   jax.ShapeDtypeStruct((M, LANE), BF16)],
        compiler_params=_params(1, 32),
        name="kv_proj",
    )(z, g_kv_l.reshape(1, kvl), w_kv, kpe_raw, cos, sin)


def _kv_cache_kernel(c_ref, w_ref, o_ref):
    o_ref[...] = _dot(c_ref[0, 0].astype(BF16), w_ref[...]).astype(o_ref.dtype)


def _kv_cache_proj(cache_ckv, w_kv, layer):
    DB, _, P, kvl = cache_ckv.shape
    N = w_kv.shape[1]
    return pl.pallas_call(
        _kv_cache_kernel,
        grid=(DB,),
        in_specs=[pl.BlockSpec((1, 1, P, kvl), lambda b: (b, layer, 0, 0)),
                  pl.BlockSpec((kvl, N), lambda b: (0, 0))],
        out_specs=pl.BlockSpec((P, N), lambda b: (b, 0)),
        out_shape=jax.ShapeDtypeStruct((DB * P, N), BF16),
        compiler_params=_params(1, 32),
        name="kv_cache_proj",
    )(cache_ckv, w_kv)


def _softmax_rows(parts):
    m = functools.reduce(jnp.maximum, [jnp.max(s, axis=-1, keepdims=True) for s in parts])
    es = [jnp.exp(s - m) for s in parts]
    tot = functools.reduce(jnp.add, [jnp.sum(e, axis=-1, keepdims=True) for e in es])
    inv = 1.0 / tot
    return [(e * inv).astype(BF16) for e in es]


def _attn_ctx_kernel(q_ref, kv_ref, kpe_ref, o_ref):
    kpe = kpe_ref[...]
    for h in range(N_HEADS):
        lo = h * HEAD_PAD
        k = jnp.concatenate([kv_ref[:, lo:lo + NOPE_DIM], kpe], axis=1)
        s = _dot_nt(q_ref[:, lo:lo + HEAD_PAD], k) * SM_SCALE
        (p,) = _softmax_rows([s])
        o_ref[:, h * V_DIM:(h + 1) * V_DIM] = _dot(p, kv_ref[:, lo + NOPE_DIM:lo + HEAD_PAD]).astype(o_ref.dtype)


def _attn_ctx(q, kv, kpe, dims):
    M = q.shape[0]
    s = dims["s_ctx"]
    n = dims["m_ctx"] // s
    dv = N_HEADS * V_DIM
    return pl.pallas_call(
        _attn_ctx_kernel,
        grid=(n,),
        in_specs=[pl.BlockSpec((s, q.shape[1]), lambda b: (b, 0)),
                  pl.BlockSpec((s, kv.shape[1]), lambda b: (b, 0)),
                  pl.BlockSpec((s, LANE), lambda b: (b, 0))],
        out_specs=pl.BlockSpec((s, dv), lambda b: (b, 0)),
        out_shape=jax.ShapeDtypeStruct((dims["m_ctx"], dv), BF16),
        compiler_params=_params(1, 32),
        name="attn_ctx",
    )(q, kv, kpe)


def _attn_lat_kernel(q_ref, kv_ref, kpe_ref, kvc_ref, kpec_ref, o_ref):
    kpe = kpe_ref[...]
    kpec = kpec_ref[0, 0].astype(BF16)
    kpec = jnp.concatenate([kpec, jnp.zeros_like(kpec)], axis=1)
    for h in range(N_HEADS):
        lo = h * HEAD_PAD
        qh = q_ref[:, lo:lo + HEAD_PAD]
        k_own = jnp.concatenate([kv_ref[:, lo:lo + NOPE_DIM], kpe], axis=1)
        k_c = jnp.concatenate([kvc_ref[:, lo:lo + NOPE_DIM], kpec], axis=1)
        s_c = _dot_nt(qh, k_c) * SM_SCALE
        s_o = _dot_nt(qh, k_own) * SM_SCALE
        p_c, p_o = _softmax_rows([s_c, s_o])
        o = _dot(p_c, kvc_ref[:, lo + NOPE_DIM:lo + HEAD_PAD]) + _dot(p_o, kv_ref[:, lo + NOPE_DIM:lo + HEAD_PAD])
        o_ref[:, h * V_DIM:(h + 1) * V_DIM] = o.astype(o_ref.dtype)


def _attn_lat(q, kv, kpe, kv_cache, cache_krope, layer, dims):
    tch, m_ctx = dims["tch"], dims["m_ctx"]
    DB, _, P, rd = cache_krope.shape
    bq = min(256, tch)
    nq = tch // bq
    dv = N_HEADS * V_DIM
    q_blk = lambda b, t: (m_ctx // bq + b * nq + t, 0)
    own_blk = lambda b, t: (m_ctx // tch + b, 0)
    return pl.pallas_call(
        _attn_lat_kernel,
        grid=(DB, nq),
        in_specs=[pl.BlockSpec((bq, q.shape[1]), q_blk),
                  pl.BlockSpec((tch, kv.shape[1]), own_blk),
                  pl.BlockSpec((tch, LANE), own_blk),
                  pl.BlockSpec((P, kv_cache.shape[1]), lambda b, t: (b, 0)),
                  pl.BlockSpec((1, 1, P, rd), lambda b, t: (b, layer, 0, 0))],
        out_specs=pl.BlockSpec((bq, dv), lambda b, t: (b * nq + t, 0)),
        out_shape=jax.ShapeDtypeStruct((DB * tch, dv), BF16),
        compiler_params=_params(2, 48),
        name="attn_lat",
    )(q, kv, kpe, kv_cache, cache_krope)


def kernel(x_prompt, x_sample, cache_ckv, cache_krope, c, c_ctx, w_ada, b_ada, g_norm1, g_norm2, w_in, w_fourier,
           pool_scale, w_pool, g_q, w_q_b, g_kv, w_kv_b, w_o_mla, w_out, w_gate, w_up, w_down, g_final):
    B, S, D = x_prompt.shape
    DB, DS, _ = x_sample.shape
    L = w_ada.shape[0]
    DF = w_fourier.shape[1]
    DP = pool_scale.shape[1]
    QL = g_q.shape[1]
    KVL = g_kv.shape[1]
    m_ctx, m_lat = B * S, DB * DS
    M = m_ctx + m_lat
    assert DS % S == 0 and m_ctx % DS == 0 and S & (S - 1) == 0 and DS & (DS - 1) == 0
    assert 1 + DB <= MOD_ROWS and DS % GRID_W == 0
    dims = dict(m_ctx=m_ctx, tch=DS, s_ctx=S)

    off_p, off_q, off_kv, off_pe = DF, DF + DP, DF + DP + QL, DF + DP + QL + KVL
    assert off_p % DP == 0 and off_q % QL == 0 and off_kv % KVL == 0 and off_pe % IN_BN == 0

    xs = (x_prompt.reshape(m_ctx, D), x_sample.reshape(m_lat, D))
    cond = jnp.zeros((MOD_ROWS, D), F32).at[0].set(c_ctx).at[1:1 + DB].set(c)
    mod4 = _ada_mod(cond, w_ada, b_ada).reshape(L, MOD_ROWS, 1, 6 * D)
    cos, sin = _rope_tables(DS)
    w_in_t = jnp.swapaxes(w_in, 1, 2)
    wf_b, wp_b, wo_b = w_fourier.astype(BF16), w_pool.reshape(L, DP, D).astype(BF16), w_o_mla.astype(BF16)
    w_out_b, w_down_b = w_out.astype(BF16), w_down.astype(BF16)

    ckv_layers, kpe_layers = [], []
    for l in range(L):
        w_q = jnp.pad(w_q_b[l].reshape(QL, N_HEADS, NOPE_DIM + ROPE_DIM),
                      ((0, 0), (0, 0), (0, HEAD_PAD - NOPE_DIM - ROPE_DIM))).reshape(QL, N_HEADS * HEAD_PAD).astype(BF16)
        w_kv = w_kv_b[l].astype(BF16)

        h = _norm_mod(xs, g_norm1[l:l + 1], mod4, l, 0, dims)
        z, kpe_raw, gates = _in_proj(h, w_in_t, l, off_pe, off_pe + ROPE_DIM, 1024)

        mixed = _dft(z, dims, DF)
        pooled = _pool(z, pool_scale[l], dims, DP, off_p // DP)
        q = _q_proj(z, g_q[l], w_q, cos, sin, dims, QL, off_q // QL)
        c_kv, kv, kpe = _kv_proj(z, kpe_raw, g_kv[l], w_kv, cos, sin, dims, KVL, off_kv // KVL)
        kv_cache = _kv_cache_proj(cache_ckv, w_kv, l)
        o_ctx = _attn_ctx(q, kv, kpe, dims)
        o_lat = _attn_lat(q, kv, kpe, kv_cache, cache_krope, l, dims)

        merged = _merge(mixed, pooled, (o_ctx, o_lat), wf_b, wp_b, wo_b, gates, l, 512, 1024)
        x = _mm_resid(merged, w_out_b, xs, mod4, l, 2, 512, 1024, dims, 48, "out_proj")

        h = _norm_mod((x,), g_norm2[l:l + 1], mod4, l, 3, dims)
        act = _swiglu_up(h, w_gate, w_up, l, 1024, 256)
        x = _mm_resid(act, w_down_b, (x,), mod4, l, 5, 512, 512, dims, 56, "down_proj")
        xs = (x,)

        ckv_layers.append(c_kv[:m_ctx].reshape(B, S, KVL))
        kpe_layers.append(kpe_raw[:m_ctx, :ROPE_DIM].reshape(B, S, ROPE_DIM))

    y_prompt = _final_norm(x, g_final, 0, m_ctx).reshape(B, S, D)
    y_sample = _final_norm(x, g_final, m_ctx, m_lat).reshape(DB, DS, D)
    return (y_prompt, y_sample, jnp.stack(ckv_layers, axis=1), jnp.stack(kpe_layers, axis=1))
```

```python
import functools

import numpy as np
import jax
import jax.numpy as jnp
from jax.experimental import pallas as pl
from jax.experimental.pallas import tpu as pltpu

N_HEADS = 16
NOPE_DIM = 128
ROPE_DIM = 64
V_DIM = 128
HEAD_PAD = 256
GRID_W = 64
EPS = 1e-6
ROPE_BASE = 10000.0
SM_SCALE = (NOPE_DIM + ROPE_DIM) ** -0.5
POOL_WINDOWS = (2, 4, 8, 16)
N_FOURIER_GROUPS = 4
MOD_ROWS = 8
LANE = 128

BF16 = jnp.bfloat16
F32 = jnp.float32


def _params(n_grid, vmem_mb):
    return pltpu.CompilerParams(dimension_semantics=("arbitrary",) * n_grid,
                                vmem_limit_bytes=vmem_mb * 2 ** 20)


def _dot(a, b):
    return jnp.dot(a, b, preferred_element_type=F32)


def _dot_nt(a, b):
    return jax.lax.dot_general(a, b, (((1,), (1,)), ((), ())), preferred_element_type=F32)


SLAB_ROWS = 256


def _slabs(n_rows):
    r = int(np.gcd(SLAB_ROWS, n_rows))
    return [slice(s * r, (s + 1) * r) for s in range(n_rows // r)]


def _mod_row(i, bm, m_ctx, tch):
    start = i * bm
    return jnp.where(start < m_ctx, 0, 1 + (start - m_ctx) // tch)


def _ada_kernel(c_ref, w_ref, b_ref, o_ref):
    c = c_ref[...]
    a = (c * jax.nn.sigmoid(c)).astype(BF16)
    o_ref[0] = _dot(a, w_ref[0].astype(BF16)) + b_ref[0]


def _ada_mod(cond, w_ada, b_ada):
    L, D, N = w_ada.shape
    bn = 512
    return pl.pallas_call(
        _ada_kernel,
        grid=(L, N // bn),
        in_specs=[pl.BlockSpec((MOD_ROWS, D), lambda l, j: (0, 0)),
                  pl.BlockSpec((1, D, bn), lambda l, j: (l, 0, j)),
                  pl.BlockSpec((1, 1, bn), lambda l, j: (l, 0, j))],
        out_specs=pl.BlockSpec((1, MOD_ROWS, bn), lambda l, j: (l, 0, j)),
        out_shape=jax.ShapeDtypeStruct((L, MOD_ROWS, N), F32),
        compiler_params=_params(2, 40),
        name="ada_mod",
    )(cond, w_ada, b_ada.reshape(L, 1, N))


def _row_specs(xs, bm, bn, n_grid):
    idx = (lambda *g: (g[-1], g[0])) if n_grid == 2 else (lambda *g: (g[-1], 0))
    if len(xs) == 1:
        return [pl.BlockSpec((bm, bn), idx)]
    n_a = xs[0].shape[0] // bm
    last_b = xs[1].shape[0] // bm - 1
    spec_a = pl.BlockSpec((bm, bn), lambda *g: (jnp.minimum(idx(*g)[0], n_a - 1), idx(*g)[1]))
    spec_b = pl.BlockSpec((bm, bn), lambda *g: (jnp.clip(idx(*g)[0] - n_a, 0, last_b), idx(*g)[1]))
    return [spec_a, spec_b]


def _norm_mod_kernel(*refs, n_x, n_a):
    x_refs, (g_ref, sh_ref, sc_ref, o_ref) = refs[:n_x], refs[n_x:]
    x = x_refs[0][...] if n_x == 1 else jnp.where(pl.program_id(0) < n_a, x_refs[0][...], x_refs[1][...])
    r = jax.lax.rsqrt(jnp.mean(x * x, axis=-1, keepdims=True) + EPS)
    y = x * r * g_ref[...]
    o_ref[...] = (y * (1.0 + sc_ref[0, 0]) + sh_ref[0, 0]).astype(o_ref.dtype)


def _norm_mod(xs, g, mod4, layer, shift_blk, dims):
    M = sum(x.shape[0] for x in xs)
    D = xs[0].shape[1]
    bm = min(256, dims["tch"])
    row = functools.partial(_mod_row, bm=bm, m_ctx=dims["m_ctx"], tch=dims["tch"])
    return pl.pallas_call(
        functools.partial(_norm_mod_kernel, n_x=len(xs), n_a=xs[0].shape[0] // bm),
        grid=(M // bm,),
        in_specs=_row_specs(xs, bm, D, 1) + [
            pl.BlockSpec((1, D), lambda i: (0, 0)),
            pl.BlockSpec((1, 1, 1, D), lambda i: (layer, row(i), 0, shift_blk)),
            pl.BlockSpec((1, 1, 1, D), lambda i: (layer, row(i), 0, shift_blk + 1))],
        out_specs=pl.BlockSpec((bm, D), lambda i: (i, 0)),
        out_shape=jax.ShapeDtypeStruct((M, D), BF16),
        compiler_params=_params(1, 32),
        name="norm_mod",
    )(*xs, g, mod4, mod4)


def _final_norm_kernel(x_ref, g_ref, o_ref):
    x = x_ref[...]
    r = jax.lax.rsqrt(jnp.mean(x * x, axis=-1, keepdims=True) + EPS)
    o_ref[...] = x * r * g_ref[...]


def _final_norm(x, g, row0, rows):
    D = x.shape[1]
    bm = int(np.gcd(256, np.gcd(rows, row0 if row0 else rows)))
    blk0 = row0 // bm
    return pl.pallas_call(
        _final_norm_kernel,
        grid=(rows // bm,),
        in_specs=[pl.BlockSpec((bm, D), lambda i: (blk0 + i, 0)),
                  pl.BlockSpec((1, D), lambda i: (0, 0))],
        out_specs=pl.BlockSpec((bm, D), lambda i: (i, 0)),
        out_shape=jax.ShapeDtypeStruct((rows, D), F32),
        compiler_params=_params(1, 32),
        name="final_norm",
    )(x, g.reshape(1, D))


IN_BN = 512


def _in_proj_kernel(h_ref, w_ref, wk_ref, z_ref, kpe_ref, g_ref, *, n_small_blk):
    j = pl.program_id(1)

    @pl.when(j < n_small_blk)
    def _():
        w = w_ref[0].astype(BF16)
        for rows in _slabs(h_ref.shape[0]):
            z_ref[rows, :] = _dot_nt(h_ref[rows, :], w)

    @pl.when(j == n_small_blk)
    def _():
        kpe_ref[...] = _dot_nt(h_ref[...], wk_ref[0].astype(BF16))

    @pl.when(j >= n_small_blk)
    def _():
        w = w_ref[0].astype(BF16)
        for rows in _slabs(h_ref.shape[0]):
            g_ref[rows, :] = jax.nn.sigmoid(_dot_nt(h_ref[rows, :], w)).astype(g_ref.dtype)


def _in_proj(h, w_t, layer, off_pe, n_small, bm):
    M, K = h.shape
    N = w_t.shape[1]
    bm = min(bm, M)
    n_small_blk = off_pe // IN_BN
    n_gate_blk = (N - n_small) // IN_BN
    assert off_pe % IN_BN == 0 and (N - n_small) % IN_BN == 0 and n_small % 8 == 0 and off_pe % LANE == 0

    def w_row(j):
        row = jnp.where(j < n_small_blk, j * IN_BN, n_small + (j - n_small_blk) * IN_BN)
        return pl.multiple_of(row, 8)

    return pl.pallas_call(
        functools.partial(_in_proj_kernel, n_small_blk=n_small_blk),
        grid=(M // bm, n_small_blk + n_gate_blk),
        in_specs=[pl.BlockSpec((bm, K), lambda i, j: (i, 0), pipeline_mode=pl.Buffered(1)),
                  pl.BlockSpec((pl.Element(1), pl.Element(IN_BN), pl.Element(K)), lambda i, j: (layer, w_row(j), 0)),
                  pl.BlockSpec((1, LANE, K), lambda i, j: (layer, off_pe // LANE, 0), pipeline_mode=pl.Buffered(1))],
        out_specs=[pl.BlockSpec((bm, IN_BN), lambda i, j: (i, jnp.minimum(j, n_small_blk - 1))),
                   pl.BlockSpec((bm, LANE), lambda i, j: (i, 0)),
                   pl.BlockSpec((bm, IN_BN), lambda i, j: (i, jnp.maximum(j - n_small_blk, 0)))],
        out_shape=[jax.ShapeDtypeStruct((M, off_pe), F32),
                   jax.ShapeDtypeStruct((M, LANE), F32),
                   jax.ShapeDtypeStruct((M, n_gate_blk * IN_BN), BF16)],
        compiler_params=_params(2, 60),
        name="in_proj",
    )(h, w_t, w_t)


def _mm_resid_kernel(a_ref, w_ref, *refs, n_x, n_a):
    x_refs, (gate_ref, o_ref) = refs[:n_x], refs[n_x:]
    first = pl.program_id(1) < n_a
    for rows in _slabs(a_ref.shape[0]):
        x = x_refs[0][rows, :] if n_x == 1 else jnp.where(first, x_refs[0][rows, :], x_refs[1][rows, :])
        o_ref[rows, :] = x + gate_ref[0, 0] * _dot(a_ref[rows, :], w_ref[0])


def _mm_resid(a, w, xs, mod4, layer, gate_blk, bm, bn, dims, vmem_mb, name):
    M, K = a.shape
    N = w.shape[2]
    bm = min(bm, dims["tch"])
    row = functools.partial(_mod_row, bm=bm, m_ctx=dims["m_ctx"], tch=dims["tch"])
    nb = N // bn
    return pl.pallas_call(
        functools.partial(_mm_resid_kernel, n_x=len(xs), n_a=xs[0].shape[0] // bm),
        grid=(nb, M // bm),
        in_specs=[pl.BlockSpec((bm, K), lambda j, i: (i, 0)),
                  pl.BlockSpec((1, K, bn), lambda j, i: (layer, 0, j))] + _row_specs(xs, bm, bn, 2) + [
                  pl.BlockSpec((1, 1, 1, bn), lambda j, i: (layer, row(i), 0, gate_blk * nb + j))],
        out_specs=pl.BlockSpec((bm, bn), lambda j, i: (i, j)),
        out_shape=jax.ShapeDtypeStruct((M, N), F32),
        compiler_params=_params(2, vmem_mb),
        name=name,
    )(a, w, *xs, mod4)


def _swiglu_kernel(a_ref, wg_ref, wu_ref, o_ref):
    wg = wg_ref[0].astype(BF16)
    wu = wu_ref[0].astype(BF16)
    for rows in _slabs(a_ref.shape[0]):
        a = a_ref[rows, :]
        g = _dot(a, wg)
        u = _dot(a, wu)
        o_ref[rows, :] = (g * jax.nn.sigmoid(g) * u).astype(o_ref.dtype)


def _swiglu_up(a, wg, wu, layer, bm, bn):
    M, K = a.shape
    N = wg.shape[2]
    bm = min(bm, M)
    return pl.pallas_call(
        _swiglu_kernel,
        grid=(M // bm, N // bn),
        in_specs=[pl.BlockSpec((bm, K), lambda i, j: (i, 0)),
                  pl.BlockSpec((1, K, bn), lambda i, j: (layer, 0, j)),
                  pl.BlockSpec((1, K, bn), lambda i, j: (layer, 0, j))],
        out_specs=pl.BlockSpec((bm, bn), lambda i, j: (i, j)),
        out_shape=jax.ShapeDtypeStruct((M, N), BF16),
        compiler_params=_params(2, 60),
        name="swiglu_up",
    )(a, wg, wu)


def _merge_kernel(af_ref, ap_ref, aoc_ref, aol_ref, wf_ref, wp_ref, wo_ref, g0_ref, g1_ref, g2_ref, o_ref, *, n_a):
    first = pl.program_id(1) < n_a
    for rows in _slabs(af_ref.shape[0]):
        ao = jnp.where(first, aoc_ref[rows, :], aol_ref[rows, :])
        y = g0_ref[rows, :].astype(F32) * _dot(af_ref[rows, :], wf_ref[0])
        y = y + g1_ref[rows, :].astype(F32) * _dot(ap_ref[rows, :], wp_ref[0])
        y = y + g2_ref[rows, :].astype(F32) * _dot(ao, wo_ref[0])
        o_ref[rows, :] = y.astype(o_ref.dtype)


def _merge(af, ap, aos, wf, wp, wo, gates, layer, bm, bn):
    M = af.shape[0]
    N = wf.shape[2]
    bm = min(bm, aos[0].shape[0], aos[1].shape[0])
    nb = N // bn
    a_spec = lambda a: pl.BlockSpec((bm, a.shape[1]), lambda j, i: (i, 0))
    w_spec = lambda w: pl.BlockSpec((1, w.shape[1], bn), lambda j, i: (layer, 0, j))
    g_spec = lambda k: pl.BlockSpec((bm, bn), lambda j, i: (i, k * nb + j))
    return pl.pallas_call(
        functools.partial(_merge_kernel, n_a=aos[0].shape[0] // bm),
        grid=(nb, M // bm),
        in_specs=[a_spec(af), a_spec(ap)] + _row_specs(aos, bm, aos[0].shape[1], 1) + [
                  w_spec(wf), w_spec(wp), w_spec(wo), g_spec(0), g_spec(1), g_spec(2)],
        out_specs=pl.BlockSpec((bm, bn), lambda j, i: (i, j)),
        out_shape=jax.ShapeDtypeStruct((M, N), BF16),
        compiler_params=_params(2, 60),
        name="merge",
    )(af, ap, *aos, wf, wp, wo, gates, gates, gates)


def _dft_kernel(f_ref, ch_ref, pc_ref, pl_ref, o_ref, *, n_ctx_chunks, s_ctx, fg):
    i = pl.program_id(0)
    tch = f_ref.shape[0]

    def mix(pos_ref, s):
        scale = 1.0 / float(np.sqrt(s * fg))
        for g in range(N_FOURIER_GROUPS):
            x = f_ref[:, g * fg:(g + 1) * fg].astype(BF16)
            xcs = _dot(x, ch_ref[...]).astype(BF16)
            for q in range(tch // s):
                rows = slice(q * s, (q + 1) * s)
                stacked = jnp.concatenate([xcs[rows, :fg], xcs[rows, fg:]], axis=0)
                y = _dot(pos_ref[...], stacked) * scale
                o_ref[rows, g * fg:(g + 1) * fg] = y.astype(o_ref.dtype)

    @pl.when(i < n_ctx_chunks)
    def _():
        mix(pc_ref, s_ctx)

    @pl.when(i >= n_ctx_chunks)
    def _():
        mix(pl_ref, tch)


def _dft_tables(s_ctx, s_lat, fg):
    def cs(n):
        k = np.arange(n, dtype=np.int64)
        ang = 2.0 * np.pi * ((k[:, None] * k[None, :]) % n).astype(np.float64) / n
        return np.cos(ang), np.sin(ang)
    cc, sc = cs(fg)
    chan = np.concatenate([cc, sc], axis=1)
    pos = []
    for s in (s_ctx, s_lat):
        c, sn = cs(s)
        pos.append(np.concatenate([c, -sn], axis=1))
    return (jnp.asarray(chan, BF16), jnp.asarray(pos[0], BF16), jnp.asarray(pos[1], BF16))


def _dft(z, dims, df):
    M = z.shape[0]
    tch, s_ctx = dims["tch"], dims["s_ctx"]
    fg = df // N_FOURIER_GROUPS
    chan, pos_c, pos_l = _dft_tables(s_ctx, tch, fg)
    full = lambda a: pl.BlockSpec(a.shape, lambda i: (0, 0))
    return pl.pallas_call(
        functools.partial(_dft_kernel, n_ctx_chunks=dims["m_ctx"] // tch, s_ctx=s_ctx, fg=fg),
        grid=(M // tch,),
        in_specs=[pl.BlockSpec((tch, df), lambda i: (i, 0)), full(chan), full(pos_c), full(pos_l)],
        out_specs=pl.BlockSpec((tch, df), lambda i: (i, 0)),
        out_shape=jax.ShapeDtypeStruct((M, df), BF16),
        compiler_params=_params(1, 48),
        name="dft_mix",
    )(z, chan, pos_c, pos_l)


def _pool_kernel(p_ref, ps_ref, o_ref, *, n_ctx_chunks, s_ctx, pg):
    i = pl.program_id(0)
    tch = p_ref.shape[0]
    s_seq = jnp.where(i < n_ctx_chunks, s_ctx, tch)
    pos = jax.lax.broadcasted_iota(jnp.int32, (tch, pg), 0) & (s_seq - 1)
    for gi, w in enumerate(POOL_WINDOWS):
        half = w // 2
        cols = slice(gi * pg, (gi + 1) * pg)
        x = p_ref[:, cols]
        tot = jnp.zeros_like(x)
        for d in range(-half, w - half):
            shifted = x if d == 0 else pltpu.roll(x, (-d) % tch, 0)
            ok = (pos + d >= 0) & (pos + d < s_seq)
            tot = tot + jnp.where(ok, shifted, 0.0)
        cnt = (jnp.minimum(pos + (w - half), s_seq) - jnp.maximum(pos - half, 0)).astype(F32)
        o_ref[:, cols] = ((tot / cnt - x) * ps_ref[:, cols]).astype(o_ref.dtype)


def _pool(z, pool_scale_l, dims, dp, col_blk):
    M = z.shape[0]
    tch = dims["tch"]
    return pl.pallas_call(
        functools.partial(_pool_kernel, n_ctx_chunks=dims["m_ctx"] // tch, s_ctx=dims["s_ctx"],
                          pg=dp // len(POOL_WINDOWS)),
        grid=(M // tch,),
        in_specs=[pl.BlockSpec((tch, dp), lambda i: (i, col_blk)),
                  pl.BlockSpec((1, dp), lambda i: (0, 0))],
        out_specs=pl.BlockSpec((tch, dp), lambda i: (i, 0)),
        out_shape=jax.ShapeDtypeStruct((M, dp), BF16),
        compiler_params=_params(1, 32),
        name="pool",
    )(z, pool_scale_l.reshape(1, dp))


def _rope_tables(n_tokens):
    rows = n_tokens // GRID_W
    row = np.repeat(np.arange(rows, dtype=np.float64), GRID_W)
    col = np.tile(np.arange(GRID_W, dtype=np.float64), rows)
    inv = ROPE_BASE ** (-np.arange(ROPE_DIM // 4, dtype=np.float64) * (4.0 / ROPE_DIM))
    ar = row[:, None] * inv[None, :]
    ac = col[:, None] * inv[None, :]
    ang = np.concatenate([ar, ar, ac, ac], axis=1)
    sign = np.tile(np.concatenate([-np.ones(ROPE_DIM // 4), np.ones(ROPE_DIM // 4)]), 2)
    cos = np.concatenate([np.cos(ang), np.ones_like(ang)], axis=1)
    sin = np.concatenate([np.sin(ang) * sign[None, :], np.zeros_like(ang)], axis=1)
    return jnp.asarray(cos, F32), jnp.asarray(sin, F32)


def _rope(x, cos, sin):
    q = ROPE_DIM // 4
    lane = jax.lax.broadcasted_iota(jnp.int32, x.shape, 1)
    first = (lane & (2 * q - 1)) < q
    swapped = jnp.where(first, pltpu.roll(x, LANE - q, 1), pltpu.roll(x, q, 1))
    return x * cos + swapped * sin


def _q_kernel(cq_ref, g_ref, w_ref, cos_ref, sin_ref, o_ref, *, n_ctx_tiles):
    i = pl.program_id(0)
    x = cq_ref[...]
    r = jax.lax.rsqrt(jnp.mean(x * x, axis=-1, keepdims=True) + EPS)
    q = _dot((x * r * g_ref[...]).astype(BF16), w_ref[...])

    @pl.when(i < n_ctx_tiles)
    def _():
        o_ref[...] = q.astype(o_ref.dtype)

    @pl.when(i >= n_ctx_tiles)
    def _():
        cos, sin = cos_ref[...], sin_ref[...]
        for h in range(N_HEADS):
            lo = h * HEAD_PAD
            o_ref[:, lo:lo + NOPE_DIM] = q[:, lo:lo + NOPE_DIM].astype(o_ref.dtype)
            pe = q[:, lo + NOPE_DIM:lo + HEAD_PAD]
            o_ref[:, lo + NOPE_DIM:lo + HEAD_PAD] = _rope(pe, cos, sin).astype(o_ref.dtype)


def _q_proj(z, g_q_l, w_q, cos, sin, dims, ql, col_blk):
    M = z.shape[0]
    N = w_q.shape[1]
    bm = min(256, dims["tch"])
    n_ctx_tiles = dims["m_ctx"] // bm
    per_seq = dims["tch"] // bm
    pos_blk = lambda i: (jnp.maximum(i - n_ctx_tiles, 0) % per_seq, 0)
    return pl.pallas_call(
        functools.partial(_q_kernel, n_ctx_tiles=n_ctx_tiles),
        grid=(M // bm,),
        in_specs=[pl.BlockSpec((bm, ql), lambda i: (i, col_blk)),
                  pl.BlockSpec((1, ql), lambda i: (0, 0)),
                  pl.BlockSpec((ql, N), lambda i: (0, 0)),
                  pl.BlockSpec((bm, LANE), pos_blk),
                  pl.BlockSpec((bm, LANE), pos_blk)],
        out_specs=pl.BlockSpec((bm, N), lambda i: (i, 0)),
        out_shape=jax.ShapeDtypeStruct((M, N), BF16),
        compiler_params=_params(1, 48),
        name="q_proj",
    )(z, g_q_l.reshape(1, ql), w_q, cos, sin)


def _kv_kernel(ckv_ref, g_ref, w_ref, kpe_ref, cos_ref, sin_ref, ckv_o, kv_o, kpe_o, *, n_ctx_tiles):
    i = pl.program_id(0)
    x = ckv_ref[...]
    r = jax.lax.rsqrt(jnp.mean(x * x, axis=-1, keepdims=True) + EPS)
    c = x * r * g_ref[...]
    ckv_o[...] = c
    kv_o[...] = _dot(c.astype(BF16), w_ref[...]).astype(kv_o.dtype)
    kpe = kpe_ref[...]
    kpe = jnp.where(jax.lax.broadcasted_iota(jnp.int32, kpe.shape, 1) < ROPE_DIM, kpe, 0.0)

    @pl.when(i < n_ctx_tiles)
    def _():
        kpe_o[...] = kpe.astype(kpe_o.dtype)

    @pl.when(i >= n_ctx_tiles)
    def _():
        kpe_o[...] = _rope(kpe, cos_ref[...], sin_ref[...]).astype(kpe_o.dtype)


def _kv_proj(z, kpe_raw, g_kv_l, w_kv, cos, sin, dims, kvl, ckv_blk):
    M = z.shape[0]
    N = w_kv.shape[1]
    bm = min(256, dims["tch"])
    n_ctx_tiles = dims["m_ctx"] // bm
    per_seq = dims["tch"] // bm
    pos_blk = lambda i: (jnp.maximum(i - n_ctx_tiles, 0) % per_seq, 0)
    return pl.pallas_call(
        functools.partial(_kv_kernel, n_ctx_tiles=n_ctx_tiles),
        grid=(M // bm,),
        in_specs=[pl.BlockSpec((bm, kvl), lambda i: (i, ckv_blk)),
                  pl.BlockSpec((1, kvl), lambda i: (0, 0)),
                  pl.BlockSpec((kvl, N), lambda i: (0, 0)),
                  pl.BlockSpec((bm, LANE), lambda i: (i, 0)),
                  pl.BlockSpec((bm, LANE), pos_blk),
                  pl.BlockSpec((bm, LANE), pos_blk)],
        out_specs=[pl.BlockSpec((bm, kvl), lambda i: (i, 0)),
                   pl.BlockSpec((bm, N), lambda i: (i, 0)),
                   pl.BlockSpec((bm, LANE), lambda i: (i, 0))],
        out_shape=[jax.ShapeDtypeStruct((M, kvl), F32),
                   jax.ShapeDtypeStruct((M, N), BF16),
                   jax.ShapeDtypeStruct((M, LANE), BF16)],
        compiler_params=_params(1, 32),
        name="kv_proj",
    )(z, g_kv_l.reshape(1, kvl), w_kv, kpe_raw, cos, sin)


def _kv_cache_kernel(c_ref, w_ref, o_ref):
    o_ref[...] = _dot(c_ref[0, 0].astype(BF16), w_ref[...]).astype(o_ref.dtype)


def _kv_cache_proj(cache_ckv, w_kv, layer):
    DB, _, P, kvl = cache_ckv.shape
    N = w_kv.shape[1]
    return pl.pallas_call(
        _kv_cache_kernel,
        grid=(DB,),
        in_specs=[pl.BlockSpec((1, 1, P, kvl), lambda b: (b, layer, 0, 0)),
                  pl.BlockSpec((kvl, N), lambda b: (0, 0))],
        out_specs=pl.BlockSpec((P, N), lambda b: (b, 0)),
        out_shape=jax.ShapeDtypeStruct((DB * P, N), BF16),
        compiler_params=_params(1, 32),
        name="kv_cache_proj",
    )(cache_ckv, w_kv)


def _softmax_rows(parts):
    m = functools.reduce(jnp.maximum, [jnp.max(s, axis=-1, keepdims=True) for s in parts])
    es = [jnp.exp(s - m) for s in parts]
    tot = functools.reduce(jnp.add, [jnp.sum(e, axis=-1, keepdims=True) for e in es])
    inv = 1.0 / tot
    return [(e * inv).astype(BF16) for e in es]


def _attn_ctx_kernel(q_ref, kv_ref, kpe_ref, o_ref):
    kpe = kpe_ref[...]
    for h in range(N_HEADS):
        lo = h * HEAD_PAD
        k = jnp.concatenate([kv_ref[:, lo:lo + NOPE_DIM], kpe], axis=1)
        s = _dot_nt(q_ref[:, lo:lo + HEAD_PAD], k) * SM_SCALE
        (p,) = _softmax_rows([s])
        o_ref[:, h * V_DIM:(h + 1) * V_DIM] = _dot(p, kv_ref[:, lo + NOPE_DIM:lo + HEAD_PAD]).astype(o_ref.dtype)


def _attn_ctx(q, kv, kpe, dims):
    M = q.shape[0]
    s = dims["s_ctx"]
    n = dims["m_ctx"] // s
    dv = N_HEADS * V_DIM
    return pl.pallas_call(
        _attn_ctx_kernel,
        grid=(n,),
        in_specs=[pl.BlockSpec((s, q.shape[1]), lambda b: (b, 0)),
                  pl.BlockSpec((s, kv.shape[1]), lambda b: (b, 0)),
                  pl.BlockSpec((s, LANE), lambda b: (b, 0))],
        out_specs=pl.BlockSpec((s, dv), lambda b: (b, 0)),
        out_shape=jax.ShapeDtypeStruct((dims["m_ctx"], dv), BF16),
        compiler_params=_params(1, 32),
        name="attn_ctx",
    )(q, kv, kpe)


def _attn_lat_kernel(q_ref, kv_ref, kpe_ref, kvc_ref, kpec_ref, o_ref):
    kpe = kpe_ref[...]
    kpec = kpec_ref[0, 0].astype(BF16)
    kpec = jnp.concatenate([kpec, jnp.zeros_like(kpec)], axis=1)
    for h in range(N_HEADS):
        lo = h * HEAD_PAD
        qh = q_ref[:, lo:lo + HEAD_PAD]
        k_own = jnp.concatenate([kv_ref[:, lo:lo + NOPE_DIM], kpe], axis=1)
        k_c = jnp.concatenate([kvc_ref[:, lo:lo + NOPE_DIM], kpec], axis=1)
        s_c = _dot_nt(qh, k_c) * SM_SCALE
        s_o = _dot_nt(qh, k_own) * SM_SCALE
        p_c, p_o = _softmax_rows([s_c, s_o])
        o = _dot(p_c, kvc_ref[:, lo + NOPE_DIM:lo + HEAD_PAD]) + _dot(p_o, kv_ref[:, lo + NOPE_DIM:lo + HEAD_PAD])
        o_ref[:, h * V_DIM:(h + 1) * V_DIM] = o.astype(o_ref.dtype)


def _attn_lat(q, kv, kpe, kv_cache, cache_krope, layer, dims):
    tch, m_ctx = dims["tch"], dims["m_ctx"]
    DB, _, P, rd = cache_krope.shape
    bq = min(256, tch)
    nq = tch // bq
    dv = N_HEADS * V_DIM
    q_blk = lambda b, t: (m_ctx // bq + b * nq + t, 0)
    own_blk = lambda b, t: (m_ctx // tch + b, 0)
    return pl.pallas_call(
        _attn_lat_kernel,
        grid=(DB, nq),
        in_specs=[pl.BlockSpec((bq, q.shape[1]), q_blk),
                  pl.BlockSpec((tch, kv.shape[1]), own_blk),
                  pl.BlockSpec((tch, LANE), own_blk),
                  pl.BlockSpec((P, kv_cache.shape[1]), lambda b, t: (b, 0)),
                  pl.BlockSpec((1, 1, P, rd), lambda b, t: (b, layer, 0, 0))],
        out_specs=pl.BlockSpec((bq, dv), lambda b, t: (b * nq + t, 0)),
        out_shape=jax.ShapeDtypeStruct((DB * tch, dv), BF16),
        compiler_params=_params(2, 48),
        name="attn_lat",
    )(q, kv, kpe, kv_cache, cache_krope)


def kernel(x_prompt, x_sample, cache_ckv, cache_krope, c, c_ctx, w_ada, b_ada, g_norm1, g_norm2, w_in, w_fourier,
           pool_scale, w_pool, g_q, w_q_b, g_kv, w_kv_b, w_o_mla, w_out, w_gate, w_up, w_down, g_final):
    B, S, D = x_prompt.shape
    DB, DS, _ = x_sample.shape
    L = w_ada.shape[0]
    DF = w_fourier.shape[1]
    DP = pool_scale.shape[1]
    QL = g_q.shape[1]
    KVL = g_kv.shape[1]
    m_ctx, m_lat = B * S, DB * DS
    M = m_ctx + m_lat
    assert DS % S == 0 and m_ctx % DS == 0 and S & (S - 1) == 0 and DS & (DS - 1) == 0
    assert 1 + DB <= MOD_ROWS and DS % GRID_W == 0
    dims = dict(m_ctx=m_ctx, tch=DS, s_ctx=S)

    off_p, off_q, off_kv, off_pe = DF, DF + DP, DF + DP + QL, DF + DP + QL + KVL
    assert off_p % DP == 0 and off_q % QL == 0 and off_kv % KVL == 0 and off_pe % IN_BN == 0

    xs = (x_prompt.reshape(m_ctx, D), x_sample.reshape(m_lat, D))
    cond = jnp.zeros((MOD_ROWS, D), F32).at[0].set(c_ctx).at[1:1 + DB].set(c)
    mod4 = _ada_mod(cond, w_ada, b_ada).reshape(L, MOD_ROWS, 1, 6 * D)
    cos, sin = _rope_tables(DS)
    w_in_t = jnp.swapaxes(w_in, 1, 2)
    wf_b, wp_b, wo_b = w_fourier.astype(BF16), w_pool.reshape(L, DP, D).astype(BF16), w_o_mla.astype(BF16)
    w_out_b, w_down_b = w_out.astype(BF16), w_down.astype(BF16)

    ckv_layers, kpe_layers = [], []
    for l in range(L):
        w_q = jnp.pad(w_q_b[l].reshape(QL, N_HEADS, NOPE_DIM + ROPE_DIM),
                      ((0, 0), (0, 0), (0, HEAD_PAD - NOPE_DIM - ROPE_DIM))).reshape(QL, N_HEADS * HEAD_PAD).astype(BF16)
        w_kv = w_kv_b[l].astype(BF16)

        h = _norm_mod(xs, g_norm1[l:l + 1], mod4, l, 0, dims)
        z, kpe_raw, gates = _in_proj(h, w_in_t, l, off_pe, off_pe + ROPE_DIM, 2048)

        mixed = _dft(z, dims, DF)
        pooled = _pool(z, pool_scale[l], dims, DP, off_p // DP)
        q = _q_proj(z, g_q[l], w_q, cos, sin, dims, QL, off_q // QL)
        c_kv, kv, kpe = _kv_proj(z, kpe_raw, g_kv[l], w_kv, cos, sin, dims, KVL, off_kv // KVL)
        kv_cache = _kv_cache_proj(cache_ckv, w_kv, l)
        o_ctx = _attn_ctx(q, kv, kpe, dims)
        o_lat = _attn_lat(q, kv, kpe, kv_cache, cache_krope, l, dims)

        merged = _merge(mixed, pooled, (o_ctx, o_lat), wf_b, wp_b, wo_b, gates, l, 1024, 1024)
        x = _mm_resid(merged, w_out_b, xs, mod4, l, 2, 1024, 1024, dims, 60, "out_proj")

        h = _norm_mod((x,), g_norm2[l:l + 1], mod4, l, 3, dims)
        act = _swiglu_up(h, w_gate, w_up, l, 2048, 256)
        x = _mm_resid(act, w_down_b, (x,), mod4, l, 5, 512, 512, dims, 56, "down_proj")
        xs = (x,)

        ckv_layers.append(c_kv[:m_ctx].reshape(B, S, KVL))
        kpe_layers.append(kpe_raw[:m_ctx, :ROPE_DIM].reshape(B, S, ROPE_DIM))

    y_prompt = _final_norm(x, g_final, 0, m_ctx).reshape(B, S, D)
    y_sample = _final_norm(x, g_final, m_ctx, m_lat).reshape(DB, DS, D)
    return (y_prompt, y_sample, jnp.stack(ckv_layers, axis=1), jnp.stack(kpe_layers, axis=1))
```

```python
import functools

import numpy as np
import jax
import jax.numpy as jnp
from jax.experimental import pallas as pl
from jax.experimental.pallas import tpu as pltpu

N_HEADS = 16
NOPE_DIM = 128
ROPE_DIM = 64
V_DIM = 128
HEAD_PAD = 256
GRID_W = 64
EPS = 1e-6
ROPE_BASE = 10000.0
SM_SCALE = (NOPE_DIM + ROPE_DIM) ** -0.5
Q_SCALE = SM_SCALE * float(np.log2(np.e))
POOL_WINDOWS = (2, 4, 8, 16)
N_FOURIER_GROUPS = 4
MOD_ROWS = 8
LANE = 128

BF16 = jnp.bfloat16
F32 = jnp.float32


def _params(n_grid, vmem_mb):
    return pltpu.CompilerParams(dimension_semantics=("arbitrary",) * n_grid,
                                vmem_limit_bytes=vmem_mb * 2 ** 20)


def _dot(a, b):
    return jnp.dot(a, b, preferred_element_type=F32)


def _dot_nt(a, b):
    return jax.lax.dot_general(a, b, (((1,), (1,)), ((), ())), preferred_element_type=F32)


SLAB_ROWS = 256


def _slabs(n_rows):
    r = int(np.gcd(SLAB_ROWS, n_rows))
    return [slice(s * r, (s + 1) * r) for s in range(n_rows // r)]


def _mod_row(i, bm, m_ctx, tch):
    start = i * bm
    return jnp.where(start < m_ctx, 0, 1 + (start - m_ctx) // tch)


def _ada_kernel(c_ref, w_ref, b_ref, o_ref):
    c = c_ref[...]
    a = (c * jax.nn.sigmoid(c)).astype(BF16)
    o_ref[0] = _dot(a, w_ref[0].astype(BF16)) + b_ref[0]


def _ada_mod(cond, w_ada, b_ada):
    L, D, N = w_ada.shape
    bn = 512
    return pl.pallas_call(
        _ada_kernel,
        grid=(L, N // bn),
        in_specs=[pl.BlockSpec((MOD_ROWS, D), lambda l, j: (0, 0)),
                  pl.BlockSpec((1, D, bn), lambda l, j: (l, 0, j)),
                  pl.BlockSpec((1, 1, bn), lambda l, j: (l, 0, j))],
        out_specs=pl.BlockSpec((1, MOD_ROWS, bn), lambda l, j: (l, 0, j)),
        out_shape=jax.ShapeDtypeStruct((L, MOD_ROWS, N), F32),
        compiler_params=_params(2, 40),
        name="ada_mod",
    )(cond, w_ada, b_ada.reshape(L, 1, N))


def _row_specs(xs, bm, bn, n_grid):
    idx = (lambda *g: (g[-1], g[0])) if n_grid == 2 else (lambda *g: (g[-1], 0))
    if len(xs) == 1:
        return [pl.BlockSpec((bm, bn), idx)]
    n_a = xs[0].shape[0] // bm
    last_b = xs[1].shape[0] // bm - 1
    spec_a = pl.BlockSpec((bm, bn), lambda *g: (jnp.minimum(idx(*g)[0], n_a - 1), idx(*g)[1]))
    spec_b = pl.BlockSpec((bm, bn), lambda *g: (jnp.clip(idx(*g)[0] - n_a, 0, last_b), idx(*g)[1]))
    return [spec_a, spec_b]


def _norm_mod_kernel(*refs, n_x, n_a):
    x_refs, (g_ref, sh_ref, sc_ref, o_ref) = refs[:n_x], refs[n_x:]
    x = x_refs[0][...] if n_x == 1 else jnp.where(pl.program_id(0) < n_a, x_refs[0][...], x_refs[1][...])
    r = jax.lax.rsqrt(jnp.mean(x * x, axis=-1, keepdims=True) + EPS)
    y = x * r * g_ref[...]
    o_ref[...] = (y * (1.0 + sc_ref[0, 0]) + sh_ref[0, 0]).astype(o_ref.dtype)


def _norm_mod(xs, g, mod4, layer, shift_blk, dims):
    M = sum(x.shape[0] for x in xs)
    D = xs[0].shape[1]
    bm = min(512, dims["tch"])
    row = functools.partial(_mod_row, bm=bm, m_ctx=dims["m_ctx"], tch=dims["tch"])
    return pl.pallas_call(
        functools.partial(_norm_mod_kernel, n_x=len(xs), n_a=xs[0].shape[0] // bm),
        grid=(M // bm,),
        in_specs=_row_specs(xs, bm, D, 1) + [
            pl.BlockSpec((1, D), lambda i: (0, 0)),
            pl.BlockSpec((1, 1, 1, D), lambda i: (layer, row(i), 0, shift_blk)),
            pl.BlockSpec((1, 1, 1, D), lambda i: (layer, row(i), 0, shift_blk + 1))],
        out_specs=pl.BlockSpec((bm, D), lambda i: (i, 0)),
        out_shape=jax.ShapeDtypeStruct((M, D), BF16),
        compiler_params=_params(1, 56),
        name="norm_mod",
    )(*xs, g, mod4, mod4)


def _final_norm_kernel(x_ref, g_ref, o_ref):
    x = x_ref[...]
    r = jax.lax.rsqrt(jnp.mean(x * x, axis=-1, keepdims=True) + EPS)
    o_ref[...] = x * r * g_ref[...]


def _final_norm(x, g, row0, rows):
    D = x.shape[1]
    bm = int(np.gcd(512, np.gcd(rows, row0 if row0 else rows)))
    blk0 = row0 // bm
    return pl.pallas_call(
        _final_norm_kernel,
        grid=(rows // bm,),
        in_specs=[pl.BlockSpec((bm, D), lambda i: (blk0 + i, 0)),
                  pl.BlockSpec((1, D), lambda i: (0, 0))],
        out_specs=pl.BlockSpec((bm, D), lambda i: (i, 0)),
        out_shape=jax.ShapeDtypeStruct((rows, D), F32),
        compiler_params=_params(1, 48),
        name="final_norm",
    )(x, g.reshape(1, D))


IN_BN = 512


def _in_proj_kernel(h_ref, w_ref, wk_ref, z_ref, kpe_ref, g_ref, *, n_small_blk):
    j = pl.program_id(1)

    @pl.when(j < n_small_blk)
    def _():
        w = w_ref[0].astype(BF16)
        for rows in _slabs(h_ref.shape[0]):
            z_ref[rows, :] = _dot_nt(h_ref[rows, :], w)

    @pl.when(j == n_small_blk)
    def _():
        kpe_ref[...] = _dot_nt(h_ref[...], wk_ref[0].astype(BF16))

    @pl.when(j >= n_small_blk)
    def _():
        w = w_ref[0].astype(BF16)
        for rows in _slabs(h_ref.shape[0]):
            g_ref[rows, :] = jax.nn.sigmoid(_dot_nt(h_ref[rows, :], w)).astype(g_ref.dtype)


def _in_proj(h, w_t, layer, off_pe, n_small, bm):
    M, K = h.shape
    N = w_t.shape[1]
    bm = min(bm, M)
    n_small_blk = off_pe // IN_BN
    n_gate_blk = (N - n_small) // IN_BN
    assert off_pe % IN_BN == 0 and (N - n_small) % IN_BN == 0 and n_small % 8 == 0 and off_pe % LANE == 0

    def w_row(j):
        row = jnp.where(j < n_small_blk, j * IN_BN, n_small + (j - n_small_blk) * IN_BN)
        return pl.multiple_of(row, 8)

    return pl.pallas_call(
        functools.partial(_in_proj_kernel, n_small_blk=n_small_blk),
        grid=(M // bm, n_small_blk + n_gate_blk),
        in_specs=[pl.BlockSpec((bm, K), lambda i, j: (i, 0), pipeline_mode=pl.Buffered(1)),
                  pl.BlockSpec((pl.Element(1), pl.Element(IN_BN), pl.Element(K)), lambda i, j: (layer, w_row(j), 0)),
                  pl.BlockSpec((1, LANE, K), lambda i, j: (layer, off_pe // LANE, 0), pipeline_mode=pl.Buffered(1))],
        out_specs=[pl.BlockSpec((bm, IN_BN), lambda i, j: (i, jnp.minimum(j, n_small_blk - 1))),
                   pl.BlockSpec((bm, LANE), lambda i, j: (i, 0)),
                   pl.BlockSpec((bm, IN_BN), lambda i, j: (i, jnp.maximum(j - n_small_blk, 0)))],
        out_shape=[jax.ShapeDtypeStruct((M, off_pe), F32),
                   jax.ShapeDtypeStruct((M, LANE), F32),
                   jax.ShapeDtypeStruct((M, n_gate_blk * IN_BN), BF16)],
        compiler_params=_params(2, 60),
        name="in_proj",
    )(h, w_t, w_t)


def _mm_resid_kernel(a_ref, w_ref, *refs, n_x, n_a):
    x_refs, (gate_ref, o_ref) = refs[:n_x], refs[n_x:]
    first = pl.program_id(1) < n_a
    for rows in _slabs(a_ref.shape[0]):
        x = x_refs[0][rows, :] if n_x == 1 else jnp.where(first, x_refs[0][rows, :], x_refs[1][rows, :])
        o_ref[rows, :] = x + gate_ref[0, 0] * _dot(a_ref[rows, :], w_ref[0])


def _mm_resid(a, w, xs, mod4, layer, gate_blk, bm, bn, dims, vmem_mb, name, mod_layer=None):
    mod_layer = layer if mod_layer is None else mod_layer
    M, K = a.shape
    N = w.shape[2]
    bm = min(bm, dims["tch"])
    row = functools.partial(_mod_row, bm=bm, m_ctx=dims["m_ctx"], tch=dims["tch"])
    nb = N // bn
    return pl.pallas_call(
        functools.partial(_mm_resid_kernel, n_x=len(xs), n_a=xs[0].shape[0] // bm),
        grid=(nb, M // bm),
        in_specs=[pl.BlockSpec((bm, K), lambda j, i: (i, 0)),
                  pl.BlockSpec((1, K, bn), lambda j, i: (layer, 0, j))] + _row_specs(xs, bm, bn, 2) + [
                  pl.BlockSpec((1, 1, 1, bn), lambda j, i: (mod_layer, row(i), 0, gate_blk * nb + j))],
        out_specs=pl.BlockSpec((bm, bn), lambda j, i: (i, j)),
        out_shape=jax.ShapeDtypeStruct((M, N), F32),
        compiler_params=_params(2, vmem_mb),
        name=name,
    )(a, w, *xs, mod4)


def _swiglu_kernel(a_ref, wg_ref, wu_ref, wd_ref, o_ref, wd_o, *, n_cast):
    wg = wg_ref[0].astype(BF16)
    wu = wu_ref[0].astype(BF16)
    for rows in _slabs(a_ref.shape[0]):
        a = a_ref[rows, :]
        g = _dot(a, wg)
        u = _dot(a, wu)
        o_ref[rows, :] = (g * jax.nn.sigmoid(g) * u).astype(o_ref.dtype)

    @pl.when(pl.program_id(0) * pl.num_programs(1) + pl.program_id(1) < n_cast)
    def _():
        wd_o[...] = wd_ref[0].astype(wd_o.dtype)


def _swiglu_up(a, wg, wu, wd, layer, bm, bn):
    M, K = a.shape
    N = wg.shape[2]
    KD, ND = wd.shape[1:]
    bm = min(bm, M)
    nj = N // bn
    n_steps = (M // bm) * nj
    n_cast = max(n for n in range(1, n_steps + 1) if KD % n == 0 and (KD // n) % 16 == 0)
    rb = KD // n_cast
    cast_blk = lambda i, j: jnp.minimum(i * nj + j, n_cast - 1)
    return pl.pallas_call(
        functools.partial(_swiglu_kernel, n_cast=n_cast),
        grid=(M // bm, nj),
        in_specs=[pl.BlockSpec((bm, K), lambda i, j: (i, 0)),
                  pl.BlockSpec((1, K, bn), lambda i, j: (layer, 0, j)),
                  pl.BlockSpec((1, K, bn), lambda i, j: (layer, 0, j)),
                  pl.BlockSpec((1, rb, ND), lambda i, j: (layer, cast_blk(i, j), 0))],
        out_specs=[pl.BlockSpec((bm, bn), lambda i, j: (i, j)),
                   pl.BlockSpec((rb, ND), lambda i, j: (cast_blk(i, j), 0))],
        out_shape=[jax.ShapeDtypeStruct((M, N), BF16),
                   jax.ShapeDtypeStruct((KD, ND), BF16)],
        compiler_params=_params(2, 60),
        name="swiglu_up",
    )(a, wg, wu, wd)


def _merge_kernel(af_ref, ap_ref, aoc_ref, aol_ref, wf_ref, wp_ref, wo_ref, g0_ref, g1_ref, g2_ref, o_ref, *, n_a):
    first = pl.program_id(1) < n_a
    for rows in _slabs(af_ref.shape[0]):
        ao = jnp.where(first, aoc_ref[rows, :], aol_ref[rows, :])
        y = g0_ref[rows, :].astype(F32) * _dot(af_ref[rows, :], wf_ref[0])
        y = y + g1_ref[rows, :].astype(F32) * _dot(ap_ref[rows, :], wp_ref[0])
        y = y + g2_ref[rows, :].astype(F32) * _dot(ao, wo_ref[0])
        o_ref[rows, :] = y.astype(o_ref.dtype)


def _merge(af, ap, aos, wf, wp, wo, gates, layer, bm, bn):
    M = af.shape[0]
    N = wf.shape[2]
    bm = min(bm, aos[0].shape[0], aos[1].shape[0])
    nb = N // bn
    a_spec = lambda a: pl.BlockSpec((bm, a.shape[1]), lambda j, i: (i, 0))
    w_spec = lambda w: pl.BlockSpec((1, w.shape[1], bn), lambda j, i: (layer, 0, j))
    g_spec = lambda k: pl.BlockSpec((bm, bn), lambda j, i: (i, k * nb + j))
    return pl.pallas_call(
        functools.partial(_merge_kernel, n_a=aos[0].shape[0] // bm),
        grid=(nb, M // bm),
        in_specs=[a_spec(af), a_spec(ap)] + _row_specs(aos, bm, aos[0].shape[1], 1) + [
                  w_spec(wf), w_spec(wp), w_spec(wo), g_spec(0), g_spec(1), g_spec(2)],
        out_specs=pl.BlockSpec((bm, bn), lambda j, i: (i, j)),
        out_shape=jax.ShapeDtypeStruct((M, N), BF16),
        compiler_params=_params(2, 60),
        name="merge",
    )(af, ap, *aos, wf, wp, wo, gates, gates, gates)


def _dft_kernel(f_ref, ch_ref, pc_ref, pl_ref, o_ref, *, n_ctx_chunks, s_ctx, fg):
    i = pl.program_id(0)
    tch = f_ref.shape[0]

    def mix(pos_ref, s):
        scale = 1.0 / float(np.sqrt(s * fg))
        for g in range(N_FOURIER_GROUPS):
            x = f_ref[:, g * fg:(g + 1) * fg].astype(BF16)
            xcs = _dot(x, ch_ref[...]).astype(BF16)
            for q in range(tch // s):
                rows = slice(q * s, (q + 1) * s)
                stacked = jnp.concatenate([xcs[rows, :fg], xcs[rows, fg:]], axis=0)
                y = _dot(pos_ref[...], stacked) * scale
                o_ref[rows, g * fg:(g + 1) * fg] = y.astype(o_ref.dtype)

    @pl.when(i < n_ctx_chunks)
    def _():
        mix(pc_ref, s_ctx)

    @pl.when(i >= n_ctx_chunks)
    def _():
        mix(pl_ref, tch)


def _dft_tables(s_ctx, s_lat, fg):
    def cs(n):
        k = np.arange(n, dtype=np.int64)
        ang = 2.0 * np.pi * ((k[:, None] * k[None, :]) % n).astype(np.float64) / n
        return np.cos(ang), np.sin(ang)
    cc, sc = cs(fg)
    chan = np.concatenate([cc, sc], axis=1)
    pos = []
    for s in (s_ctx, s_lat):
        c, sn = cs(s)
        pos.append(np.concatenate([c, -sn], axis=1))
    return (jnp.asarray(chan, BF16), jnp.asarray(pos[0], BF16), jnp.asarray(pos[1], BF16))


def _dft(z, dims, df):
    M = z.shape[0]
    tch, s_ctx = dims["tch"], dims["s_ctx"]
    fg = df // N_FOURIER_GROUPS
    chan, pos_c, pos_l = _dft_tables(s_ctx, tch, fg)
    full = lambda a: pl.BlockSpec(a.shape, lambda i: (0, 0))
    return pl.pallas_call(
        functools.partial(_dft_kernel, n_ctx_chunks=dims["m_ctx"] // tch, s_ctx=s_ctx, fg=fg),
        grid=(M // tch,),
        in_specs=[pl.BlockSpec((tch, df), lambda i: (i, 0)), full(chan), full(pos_c), full(pos_l)],
        out_specs=pl.BlockSpec((tch, df), lambda i: (i, 0)),
        out_shape=jax.ShapeDtypeStruct((M, df), BF16),
        compiler_params=_params(1, 48),
        name="dft_mix",
    )(z, chan, pos_c, pos_l)


def _pool_kernel(p_ref, ps_ref, o_ref, *, n_ctx_chunks, s_ctx, pg):
    i = pl.program_id(0)
    tch = p_ref.shape[0]
    s_seq = jnp.where(i < n_ctx_chunks, s_ctx, tch)
    pos = jax.lax.broadcasted_iota(jnp.int32, (tch, pg), 0) & (s_seq - 1)
    for gi, w in enumerate(POOL_WINDOWS):
        half = w // 2
        cols = slice(gi * pg, (gi + 1) * pg)
        x = p_ref[:, cols]
        tot = jnp.zeros_like(x)
        for d in range(-half, w - half):
            shifted = x if d == 0 else pltpu.roll(x, (-d) % tch, 0)
            ok = (pos + d >= 0) & (pos + d < s_seq)
            tot = tot + jnp.where(ok, shifted, 0.0)
        cnt = (jnp.minimum(pos + (w - half), s_seq) - jnp.maximum(pos - half, 0)).astype(F32)
        o_ref[:, cols] = ((tot / cnt - x) * ps_ref[:, cols]).astype(o_ref.dtype)


def _pool(z, pool_scale_l, dims, dp, col_blk):
    M = z.shape[0]
    tch = dims["tch"]
    return pl.pallas_call(
        functools.partial(_pool_kernel, n_ctx_chunks=dims["m_ctx"] // tch, s_ctx=dims["s_ctx"],
                          pg=dp // len(POOL_WINDOWS)),
        grid=(M // tch,),
        in_specs=[pl.BlockSpec((tch, dp), lambda i: (i, col_blk)),
                  pl.BlockSpec((1, dp), lambda i: (0, 0))],
        out_specs=pl.BlockSpec((tch, dp), lambda i: (i, 0)),
        out_shape=jax.ShapeDtypeStruct((M, dp), BF16),
        compiler_params=_params(1, 32),
        name="pool",
    )(z, pool_scale_l.reshape(1, dp))


def _rope_tables(n_tokens):
    rows = n_tokens // GRID_W
    row = np.repeat(np.arange(rows, dtype=np.float64), GRID_W)
    col = np.tile(np.arange(GRID_W, dtype=np.float64), rows)
    inv = ROPE_BASE ** (-np.arange(ROPE_DIM // 4, dtype=np.float64) * (4.0 / ROPE_DIM))
    ar = row[:, None] * inv[None, :]
    ac = col[:, None] * inv[None, :]
    ang = np.concatenate([ar, ar, ac, ac], axis=1)
    sign = np.tile(np.concatenate([-np.ones(ROPE_DIM // 4), np.ones(ROPE_DIM // 4)]), 2)
    cos = np.concatenate([np.cos(ang), np.ones_like(ang)], axis=1)
    sin = np.concatenate([np.sin(ang) * sign[None, :], np.zeros_like(ang)], axis=1)
    return jnp.asarray(cos, F32), jnp.asarray(sin, F32)


def _rope(x, cos, sin):
    q = ROPE_DIM // 4
    lane = jax.lax.broadcasted_iota(jnp.int32, x.shape, 1)
    first = (lane & (2 * q - 1)) < q
    swapped = jnp.where(first, pltpu.roll(x, LANE - q, 1), pltpu.roll(x, q, 1))
    return x * cos + swapped * sin


def _q_kernel(cq_ref, g_ref, w_ref, cos_ref, sin_ref, o_ref, *, n_ctx_tiles):
    i = pl.program_id(0)
    x = cq_ref[...]
    r = jax.lax.rsqrt(jnp.mean(x * x, axis=-1, keepdims=True) + EPS)
    q = _dot((x * r * g_ref[...]).astype(BF16), w_ref[...]) * Q_SCALE

    @pl.when(i < n_ctx_tiles)
    def _():
        o_ref[...] = q.astype(o_ref.dtype)

    @pl.when(i >= n_ctx_tiles)
    def _():
        cos, sin = cos_ref[...], sin_ref[...]
        for h in range(N_HEADS):
            lo = h * HEAD_PAD
            o_ref[:, lo:lo + NOPE_DIM] = q[:, lo:lo + NOPE_DIM].astype(o_ref.dtype)
            pe = q[:, lo + NOPE_DIM:lo + HEAD_PAD]
            o_ref[:, lo + NOPE_DIM:lo + HEAD_PAD] = _rope(pe, cos, sin).astype(o_ref.dtype)


def _q_proj(z, g_q_l, w_q, cos, sin, dims, ql, col_blk):
    M = z.shape[0]
    N = w_q.shape[1]
    bm = min(512, dims["tch"])
    n_ctx_tiles = dims["m_ctx"] // bm
    per_seq = dims["tch"] // bm
    pos_blk = lambda i: (jnp.maximum(i - n_ctx_tiles, 0) % per_seq, 0)
    return pl.pallas_call(
        functools.partial(_q_kernel, n_ctx_tiles=n_ctx_tiles),
        grid=(M // bm,),
        in_specs=[pl.BlockSpec((bm, ql), lambda i: (i, col_blk)),
                  pl.BlockSpec((1, ql), lambda i: (0, 0)),
                  pl.BlockSpec((ql, N), lambda i: (0, 0)),
                  pl.BlockSpec((bm, LANE), pos_blk),
                  pl.BlockSpec((bm, LANE), pos_blk)],
        out_specs=pl.BlockSpec((bm, N), lambda i: (i, 0)),
        out_shape=jax.ShapeDtypeStruct((M, N), BF16),
        compiler_params=_params(1, 48),
        name="q_proj",
    )(z, g_q_l.reshape(1, ql), w_q, cos, sin)


def _kv_kernel(ckv_ref, g_ref, w_ref, kpe_ref, cos_ref, sin_ref, ckv_o, kv_o, kpe_o, *, n_ctx_tiles):
    i = pl.program_id(0)
    x = ckv_ref[...]
    r = jax.lax.rsqrt(jnp.mean(x * x, axis=-1, keepdims=True) + EPS)
    c = x * r * g_ref[...]
    ckv_o[...] = c
    kv_o[...] = _dot(c.astype(BF16), w_ref[...]).astype(kv_o.dtype)
    kpe = kpe_ref[...]
    kpe = jnp.where(jax.lax.broadcasted_iota(jnp.int32, kpe.shape, 1) < ROPE_DIM, kpe, 0.0)

    @pl.when(i < n_ctx_tiles)
    def _():
        kpe_o[...] = kpe.astype(kpe_o.dtype)

    @pl.when(i >= n_ctx_tiles)
    def _():
        kpe_o[...] = _rope(kpe, cos_ref[...], sin_ref[...]).astype(kpe_o.dtype)


def _kv_proj(z, kpe_raw, g_kv_l, w_kv, cos, sin, dims, kvl, ckv_blk):
    M = z.shape[0]
    N = w_kv.shape[1]
    bm = min(512, dims["tch"])
    n_ctx_tiles = dims["m_ctx"] // bm
    per_seq = dims["tch"] // bm
    pos_blk = lambda i: (jnp.maximum(i - n_ctx_tiles, 0) % per_seq, 0)
    return pl.pallas_call(
        functools.partial(_kv_kernel, n_ctx_tiles=n_ctx_tiles),
        grid=(M // bm,),
        in_specs=[pl.BlockSpec((bm, kvl), lambda i: (i, ckv_blk)),
                  pl.BlockSpec((1, kvl), lambda i: (0, 0)),
                  pl.BlockSpec((kvl, N), lambda i: (0, 0)),
                  pl.BlockSpec((bm, LANE), lambda i: (i, 0)),
                  pl.BlockSpec((bm, LANE), pos_blk),
                  pl.BlockSpec((bm, LANE), pos_blk)],
        out_specs=[pl.BlockSpec((bm, kvl), lambda i: (i, 0)),
                   pl.BlockSpec((bm, N), lambda i: (i, 0)),
                   pl.BlockSpec((bm, LANE), lambda i: (i, 0))],
        out_shape=[jax.ShapeDtypeStruct((M, kvl), F32),
                   jax.ShapeDtypeStruct((M, N), BF16),
                   jax.ShapeDtypeStruct((M, LANE), BF16)],
        compiler_params=_params(1, 48),
        name="kv_proj",
    )(z, g_kv_l.reshape(1, kvl), w_kv, kpe_raw, cos, sin)


def _kv_cache_kernel(c_ref, w_ref, o_ref):
    o_ref[...] = _dot(c_ref[0, 0].astype(BF16), w_ref[...]).astype(o_ref.dtype)


def _kv_cache_proj(cache_ckv, w_kv, layer):
    DB, _, P, kvl = cache_ckv.shape
    N = w_kv.shape[1]
    return pl.pallas_call(
        _kv_cache_kernel,
        grid=(DB,),
        in_specs=[pl.BlockSpec((1, 1, P, kvl), lambda b: (b, layer, 0, 0)),
                  pl.BlockSpec((kvl, N), lambda b: (0, 0))],
        out_specs=pl.BlockSpec((P, N), lambda b: (b, 0)),
        out_shape=jax.ShapeDtypeStruct((DB * P, N), BF16),
        compiler_params=_params(1, 32),
        name="kv_cache_proj",
    )(cache_ckv, w_kv)


def _softmax_parts(parts):
    m = functools.reduce(jnp.maximum, [jnp.max(s, axis=-1, keepdims=True) for s in parts])
    es = [jnp.exp2(s - m) for s in parts]
    tot = functools.reduce(jnp.add, [jnp.sum(e, axis=-1, keepdims=True) for e in es])
    return [e.astype(BF16) for e in es], 1.0 / tot


def _attn_ctx_kernel(q_ref, kv_ref, kpe_ref, o_ref):
    kpe = kpe_ref[...]
    for h in range(N_HEADS):
        lo = h * HEAD_PAD
        k = jnp.concatenate([kv_ref[:, lo:lo + NOPE_DIM], kpe], axis=1)
        s = _dot_nt(q_ref[:, lo:lo + HEAD_PAD], k)
        (p,), inv = _softmax_parts([s])
        o = _dot(p, kv_ref[:, lo + NOPE_DIM:lo + HEAD_PAD]) * inv
        o_ref[:, h * V_DIM:(h + 1) * V_DIM] = o.astype(o_ref.dtype)


def _attn_ctx(q, kv, kpe, dims):
    M = q.shape[0]
    s = dims["s_ctx"]
    n = dims["m_ctx"] // s
    dv = N_HEADS * V_DIM
    return pl.pallas_call(
        _attn_ctx_kernel,
        grid=(n,),
        in_specs=[pl.BlockSpec((s, q.shape[1]), lambda b: (b, 0)),
                  pl.BlockSpec((s, kv.shape[1]), lambda b: (b, 0)),
                  pl.BlockSpec((s, LANE), lambda b: (b, 0))],
        out_specs=pl.BlockSpec((s, dv), lambda b: (b, 0)),
        out_shape=jax.ShapeDtypeStruct((dims["m_ctx"], dv), BF16),
        compiler_params=_params(1, 32),
        name="attn_ctx",
    )(q, kv, kpe)


def _attn_lat_kernel(q_ref, kv_ref, kpe_ref, kvc_ref, kpec_ref, o_ref):
    kpe = kpe_ref[...]
    kpec = kpec_ref[0, 0].astype(BF16)
    kpec = jnp.concatenate([kpec, jnp.zeros_like(kpec)], axis=1)
    for h in range(N_HEADS):
        lo = h * HEAD_PAD
        qh = q_ref[:, lo:lo + HEAD_PAD]
        k_own = jnp.concatenate([kv_ref[:, lo:lo + NOPE_DIM], kpe], axis=1)
        k_c = jnp.concatenate([kvc_ref[:, lo:lo + NOPE_DIM], kpec], axis=1)
        (p_c, p_o), inv = _softmax_parts([_dot_nt(qh, k_c), _dot_nt(qh, k_own)])
        o = _dot(p_c, kvc_ref[:, lo + NOPE_DIM:lo + HEAD_PAD]) + _dot(p_o, kv_ref[:, lo + NOPE_DIM:lo + HEAD_PAD])
        o_ref[:, h * V_DIM:(h + 1) * V_DIM] = (o * inv).astype(o_ref.dtype)


def _attn_lat(q, kv, kpe, kv_cache, cache_krope, layer, dims):
    tch, m_ctx = dims["tch"], dims["m_ctx"]
    DB, _, P, rd = cache_krope.shape
    bq = min(256, tch)
    nq = tch // bq
    dv = N_HEADS * V_DIM
    q_blk = lambda b, t: (m_ctx // bq + b * nq + t, 0)
    own_blk = lambda b, t: (m_ctx // tch + b, 0)
    return pl.pallas_call(
        _attn_lat_kernel,
        grid=(DB, nq),
        in_specs=[pl.BlockSpec((bq, q.shape[1]), q_blk),
                  pl.BlockSpec((tch, kv.shape[1]), own_blk),
                  pl.BlockSpec((tch, LANE), own_blk),
                  pl.BlockSpec((P, kv_cache.shape[1]), lambda b, t: (b, 0)),
                  pl.BlockSpec((1, 1, P, rd), lambda b, t: (b, layer, 0, 0))],
        out_specs=pl.BlockSpec((bq, dv), lambda b, t: (b * nq + t, 0)),
        out_shape=jax.ShapeDtypeStruct((DB * tch, dv), BF16),
        compiler_params=_params(2, 48),
        name="attn_lat",
    )(q, kv, kpe, kv_cache, cache_krope)


def kernel(x_prompt, x_sample, cache_ckv, cache_krope, c, c_ctx, w_ada, b_ada, g_norm1, g_norm2, w_in, w_fourier,
           pool_scale, w_pool, g_q, w_q_b, g_kv, w_kv_b, w_o_mla, w_out, w_gate, w_up, w_down, g_final):
    B, S, D = x_prompt.shape
    DB, DS, _ = x_sample.shape
    L = w_ada.shape[0]
    DF = w_fourier.shape[1]
    DP = pool_scale.shape[1]
    QL = g_q.shape[1]
    KVL = g_kv.shape[1]
    m_ctx, m_lat = B * S, DB * DS
    M = m_ctx + m_lat
    assert DS % S == 0 and m_ctx % DS == 0 and S & (S - 1) == 0 and DS & (DS - 1) == 0
    assert 1 + DB <= MOD_ROWS and DS % GRID_W == 0
    dims = dict(m_ctx=m_ctx, tch=DS, s_ctx=S)

    off_p, off_q, off_kv, off_pe = DF, DF + DP, DF + DP + QL, DF + DP + QL + KVL
    assert off_p % DP == 0 and off_q % QL == 0 and off_kv % KVL == 0 and off_pe % IN_BN == 0

    xs = (x_prompt.reshape(m_ctx, D), x_sample.reshape(m_lat, D))
    cond = jnp.zeros((MOD_ROWS, D), F32).at[0].set(c_ctx).at[1:1 + DB].set(c)
    mod4 = _ada_mod(cond, w_ada, b_ada).reshape(L, MOD_ROWS, 1, 6 * D)
    cos, sin = _rope_tables(DS)
    w_in_t = jnp.swapaxes(w_in, 1, 2)
    wf_b, wp_b, wo_b = w_fourier.astype(BF16), w_pool.reshape(L, DP, D).astype(BF16), w_o_mla.astype(BF16)
    w_out_b = w_out.astype(BF16)

    ckv_layers, kpe_layers = [], []
    for l in range(L):
        w_q = jnp.pad(w_q_b[l].reshape(QL, N_HEADS, NOPE_DIM + ROPE_DIM),
                      ((0, 0), (0, 0), (0, HEAD_PAD - NOPE_DIM - ROPE_DIM))).reshape(QL, N_HEADS * HEAD_PAD).astype(BF16)
        w_kv = w_kv_b[l].astype(BF16)

        h = _norm_mod(xs, g_norm1[l:l + 1], mod4, l, 0, dims)
        z, kpe_raw, gates = _in_proj(h, w_in_t, l, off_pe, off_pe + ROPE_DIM, 2048)

        mixed = _dft(z, dims, DF)
        pooled = _pool(z, pool_scale[l], dims, DP, off_p // DP)
        q = _q_proj(z, g_q[l], w_q, cos, sin, dims, QL, off_q // QL)
        c_kv, kv, kpe = _kv_proj(z, kpe_raw, g_kv[l], w_kv, cos, sin, dims, KVL, off_kv // KVL)
        kv_cache = _kv_cache_proj(cache_ckv, w_kv, l)
        o_ctx = _attn_ctx(q, kv, kpe, dims)
        o_lat = _attn_lat(q, kv, kpe, kv_cache, cache_krope, l, dims)

        merged = _merge(mixed, pooled, (o_ctx, o_lat), wf_b, wp_b, wo_b, gates, l, 1024, 1024)
        x = _mm_resid(merged, w_out_b, xs, mod4, l, 2, 1024, 1024, dims, 60, "out_proj")

        h = _norm_mod((x,), g_norm2[l:l + 1], mod4, l, 3, dims)
        act, w_down_b = _swiglu_up(h, w_gate, w_up, w_down, l, 2048, 256)
        x = _mm_resid(act, w_down_b[None], (x,), mod4, 0, 5, 512, 512, dims, 56, "down_proj", mod_layer=l)
        xs = (x,)

        ckv_layers.append(c_kv[:m_ctx].reshape(B, S, KVL))
        kpe_layers.append(kpe_raw[:m_ctx, :ROPE_DIM].reshape(B, S, ROPE_DIM))

    y_prompt = _final_norm(x, g_final, 0, m_ctx).reshape(B, S, D)
    y_sample = _final_norm(x, g_final, m_ctx, m_lat).reshape(DB, DS, D)
    return (y_prompt, y_sample, jnp.stack(ckv_layers, axis=1), jnp.stack(kpe_layers, axis=1))
```

```python
import functools

import numpy as np
import jax
import jax.numpy as jnp
from jax.experimental import pallas as pl
from jax.experimental.pallas import tpu as pltpu

N_HEADS = 16
NOPE_DIM = 128
ROPE_DIM = 64
V_DIM = 128
HEAD_PAD = 256
GRID_W = 64
EPS = 1e-6
ROPE_BASE = 10000.0
SM_SCALE = (NOPE_DIM + ROPE_DIM) ** -0.5
Q_SCALE = SM_SCALE * float(np.log2(np.e))
POOL_WINDOWS = (2, 4, 8, 16)
N_FOURIER_GROUPS = 4
MOD_ROWS = 8
LANE = 128

BF16 = jnp.bfloat16
F32 = jnp.float32


def _params(n_grid, vmem_mb):
    return pltpu.CompilerParams(dimension_semantics=("arbitrary",) * n_grid,
                                vmem_limit_bytes=vmem_mb * 2 ** 20)


def _dot(a, b):
    return jnp.dot(a, b, preferred_element_type=F32)


def _dot_nt(a, b):
    return jax.lax.dot_general(a, b, (((1,), (1,)), ((), ())), preferred_element_type=F32)


SLAB_ROWS = 256


def _slabs(n_rows, slab_rows=SLAB_ROWS):
    r = int(np.gcd(slab_rows, n_rows))
    return [slice(s * r, (s + 1) * r) for s in range(n_rows // r)]


def _mod_row(i, bm, m_ctx, tch):
    start = i * bm
    return jnp.where(start < m_ctx, 0, 1 + (start - m_ctx) // tch)


def _ada_kernel(c_ref, w_ref, b_ref, o_ref):
    c = c_ref[...]
    a = (c * jax.nn.sigmoid(c)).astype(BF16)
    o_ref[0] = _dot(a, w_ref[0].astype(BF16)) + b_ref[0]


def _ada_mod(cond, w_ada, b_ada3, layer):
    _, D, N = w_ada.shape
    bn = 512
    return pl.pallas_call(
        _ada_kernel,
        grid=(N // bn,),
        in_specs=[pl.BlockSpec((MOD_ROWS, D), lambda j: (0, 0)),
                  pl.BlockSpec((1, D, bn), lambda j: (layer, 0, j)),
                  pl.BlockSpec((1, 1, bn), lambda j: (layer, 0, j))],
        out_specs=pl.BlockSpec((1, MOD_ROWS, bn), lambda j: (0, 0, j)),
        out_shape=jax.ShapeDtypeStruct((1, MOD_ROWS, N), F32),
        compiler_params=_params(1, 40),
        name="ada_mod",
    )(cond, w_ada, b_ada3)


def _ada_side_specs(cond, w_ada, b_ada3, layer, n_steps, step_of):
    _, D, N = w_ada.shape
    bn = min(t for t in range(LANE, N + 1, LANE) if N % t == 0 and N // t <= n_steps)
    n_side = N // bn
    blk = lambda *g: jnp.minimum(step_of(*g), n_side - 1)
    in_specs = [pl.BlockSpec((MOD_ROWS, D), lambda *g: (0, 0)),
                pl.BlockSpec((1, D, bn), lambda *g: (layer, 0, blk(*g))),
                pl.BlockSpec((1, 1, bn), lambda *g: (layer, 0, blk(*g)))]
    out_spec = pl.BlockSpec((1, MOD_ROWS, bn), lambda *g: (0, 0, blk(*g)))
    out_shape = jax.ShapeDtypeStruct((1, MOD_ROWS, N), F32)
    return n_side, in_specs, out_spec, out_shape


def _row_specs(xs, bm, bn, n_grid):
    idx = (lambda *g: (g[-1], g[0])) if n_grid == 2 else (lambda *g: (g[-1], 0))
    if len(xs) == 1:
        return [pl.BlockSpec((bm, bn), idx)]
    n_a = xs[0].shape[0] // bm
    last_b = xs[1].shape[0] // bm - 1
    spec_a = pl.BlockSpec((bm, bn), lambda *g: (jnp.minimum(idx(*g)[0], n_a - 1), idx(*g)[1]))
    spec_b = pl.BlockSpec((bm, bn), lambda *g: (jnp.clip(idx(*g)[0] - n_a, 0, last_b), idx(*g)[1]))
    return [spec_a, spec_b]


def _norm_mod_kernel(*refs, n_x, n_a):
    x_refs, (g_ref, sh_ref, sc_ref, o_ref) = refs[:n_x], refs[n_x:]
    x = x_refs[0][...] if n_x == 1 else jnp.where(pl.program_id(0) < n_a, x_refs[0][...], x_refs[1][...])
    r = jax.lax.rsqrt(jnp.mean(x * x, axis=-1, keepdims=True) + EPS)
    y = x * r * g_ref[...]
    o_ref[...] = (y * (1.0 + sc_ref[0, 0]) + sh_ref[0, 0]).astype(o_ref.dtype)


def _norm_mod(xs, g, mod4, shift_blk, dims):
    M = sum(x.shape[0] for x in xs)
    D = xs[0].shape[1]
    bm = min(512, dims["tch"])
    row = functools.partial(_mod_row, bm=bm, m_ctx=dims["m_ctx"], tch=dims["tch"])
    return pl.pallas_call(
        functools.partial(_norm_mod_kernel, n_x=len(xs), n_a=xs[0].shape[0] // bm),
        grid=(M // bm,),
        in_specs=_row_specs(xs, bm, D, 1) + [
            pl.BlockSpec((1, D), lambda i: (0, 0)),
            pl.BlockSpec((1, 1, 1, D), lambda i: (0, row(i), 0, shift_blk)),
            pl.BlockSpec((1, 1, 1, D), lambda i: (0, row(i), 0, shift_blk + 1))],
        out_specs=pl.BlockSpec((bm, D), lambda i: (i, 0)),
        out_shape=jax.ShapeDtypeStruct((M, D), BF16),
        compiler_params=_params(1, 56),
        name="norm_mod",
    )(*xs, g, mod4, mod4)


def _final_norm_kernel(x_ref, g_ref, o_ref):
    x = x_ref[...]
    r = jax.lax.rsqrt(jnp.mean(x * x, axis=-1, keepdims=True) + EPS)
    o_ref[...] = x * r * g_ref[...]


def _final_norm(x, g, row0, rows):
    D = x.shape[1]
    bm = int(np.gcd(512, np.gcd(rows, row0 if row0 else rows)))
    blk0 = row0 // bm
    return pl.pallas_call(
        _final_norm_kernel,
        grid=(rows // bm,),
        in_specs=[pl.BlockSpec((bm, D), lambda i: (blk0 + i, 0)),
                  pl.BlockSpec((1, D), lambda i: (0, 0))],
        out_specs=pl.BlockSpec((bm, D), lambda i: (i, 0)),
        out_shape=jax.ShapeDtypeStruct((rows, D), F32),
        compiler_params=_params(1, 48),
        name="final_norm",
    )(x, g.reshape(1, D))


IN_BN = 512
IN_SLAB_ROWS = 512


def _in_proj_kernel(h_ref, w_ref, wk_ref, z_ref, kpe_ref, g_ref, *, n_small_blk):
    j = pl.program_id(1)

    @pl.when(j < n_small_blk)
    def _():
        w = w_ref[0].astype(BF16)
        for rows in _slabs(h_ref.shape[0], IN_SLAB_ROWS):
            z_ref[rows, :] = _dot_nt(h_ref[rows, :], w)

    @pl.when(j == n_small_blk)
    def _():
        kpe_ref[...] = _dot_nt(h_ref[...], wk_ref[0].astype(BF16))

    @pl.when(j >= n_small_blk)
    def _():
        w = w_ref[0].astype(BF16)
        for rows in _slabs(h_ref.shape[0], IN_SLAB_ROWS):
            g_ref[rows, :] = jax.nn.sigmoid(_dot_nt(h_ref[rows, :], w)).astype(g_ref.dtype)


def _in_proj(h, w_t, layer, off_pe, n_small, bm):
    M, K = h.shape
    N = w_t.shape[1]
    bm = min(bm, M)
    n_small_blk = off_pe // IN_BN
    n_gate_blk = (N - n_small) // IN_BN
    assert off_pe % IN_BN == 0 and (N - n_small) % IN_BN == 0 and n_small % 8 == 0 and off_pe % LANE == 0

    def w_row(j):
        row = jnp.where(j < n_small_blk, j * IN_BN, n_small + (j - n_small_blk) * IN_BN)
        return pl.multiple_of(row, 8)

    return pl.pallas_call(
        functools.partial(_in_proj_kernel, n_small_blk=n_small_blk),
        grid=(M // bm, n_small_blk + n_gate_blk),
        in_specs=[pl.BlockSpec((bm, K), lambda i, j: (i, 0), pipeline_mode=pl.Buffered(1)),
                  pl.BlockSpec((pl.Element(1), pl.Element(IN_BN), pl.Element(K)), lambda i, j: (layer, w_row(j), 0)),
                  pl.BlockSpec((1, LANE, K), lambda i, j: (layer, off_pe // LANE, 0), pipeline_mode=pl.Buffered(1))],
        out_specs=[pl.BlockSpec((bm, IN_BN), lambda i, j: (i, jnp.minimum(j, n_small_blk - 1))),
                   pl.BlockSpec((bm, LANE), lambda i, j: (i, 0)),
                   pl.BlockSpec((bm, IN_BN), lambda i, j: (i, jnp.maximum(j - n_small_blk, 0)))],
        out_shape=[jax.ShapeDtypeStruct((M, off_pe), F32),
                   jax.ShapeDtypeStruct((M, LANE), F32),
                   jax.ShapeDtypeStruct((M, n_gate_blk * IN_BN), BF16)],
        compiler_params=_params(2, 60),
        name="in_proj",
    )(h, w_t, w_t)


def _mm_resid_kernel(a_ref, w_ref, *refs, n_x, n_a, n_side):
    x_refs, rest = refs[:n_x], refs[n_x:]
    gate_ref, o_ref = rest[0], rest[-2 if n_side else -1]
    first = pl.program_id(1) < n_a
    for rows in _slabs(a_ref.shape[0]):
        x = x_refs[0][rows, :] if n_x == 1 else jnp.where(first, x_refs[0][rows, :], x_refs[1][rows, :])
        o_ref[rows, :] = x + gate_ref[0, 0] * _dot(a_ref[rows, :], w_ref[0])

    if n_side:
        c_ref, wa_ref, ba_ref, mod_o = rest[1], rest[2], rest[3], rest[-1]

        @pl.when(pl.program_id(0) * pl.num_programs(1) + pl.program_id(1) < n_side)
        def _():
            _ada_kernel(c_ref, wa_ref, ba_ref, mod_o)


def _mm_resid(a, w, xs, mod4, layer, gate_blk, bm, bn, dims, vmem_mb, name, ada_side=None):
    M, K = a.shape
    N = w.shape[2]
    bm = min(bm, dims["tch"])
    row = functools.partial(_mod_row, bm=bm, m_ctx=dims["m_ctx"], tch=dims["tch"])
    nb, ni = N // bn, M // bm
    in_specs = [pl.BlockSpec((bm, K), lambda j, i: (i, 0)),
                pl.BlockSpec((1, K, bn), lambda j, i: (layer, 0, j))] + _row_specs(xs, bm, bn, 2) + [
                pl.BlockSpec((1, 1, 1, bn), lambda j, i: (0, row(i), 0, gate_blk * nb + j))]
    out_specs = [pl.BlockSpec((bm, bn), lambda j, i: (i, j))]
    out_shape = [jax.ShapeDtypeStruct((M, N), F32)]
    args = [a, w, *xs, mod4]
    n_side = 0
    if ada_side is not None:
        n_side, side_in, side_out, side_shape = _ada_side_specs(*ada_side, nb * ni, lambda j, i: j * ni + i)
        in_specs += side_in
        out_specs.append(side_out)
        out_shape.append(side_shape)
        args += list(ada_side[:3])
    res = pl.pallas_call(
        functools.partial(_mm_resid_kernel, n_x=len(xs), n_a=xs[0].shape[0] // bm, n_side=n_side),
        grid=(nb, ni),
        in_specs=in_specs,
        out_specs=out_specs,
        out_shape=out_shape,
        compiler_params=_params(2, vmem_mb),
        name=name,
    )(*args)
    return res if n_side else res[0]


def _swiglu_kernel(a_ref, wg_ref, wu_ref, wd_ref, o_ref, wd_o, *, n_cast):
    wg = wg_ref[0].astype(BF16)
    wu = wu_ref[0].astype(BF16)
    for rows in _slabs(a_ref.shape[0]):
        a = a_ref[rows, :]
        g = _dot(a, wg)
        u = _dot(a, wu)
        o_ref[rows, :] = (g * jax.nn.sigmoid(g) * u).astype(o_ref.dtype)

    @pl.when(pl.program_id(0) * pl.num_programs(1) + pl.program_id(1) < n_cast)
    def _():
        wd_o[...] = wd_ref[0].astype(wd_o.dtype)


def _swiglu_up(a, wg, wu, wd, layer, bm, bn):
    M, K = a.shape
    N = wg.shape[2]
    KD, ND = wd.shape[1:]
    bm = min(bm, M)
    nj = N // bn
    n_steps = (M // bm) * nj
    n_cast = max(n for n in range(1, n_steps + 1) if KD % n == 0 and (KD // n) % 16 == 0)
    rb = KD // n_cast
    cast_blk = lambda i, j: jnp.minimum(i * nj + j, n_cast - 1)
    return pl.pallas_call(
        functools.partial(_swiglu_kernel, n_cast=n_cast),
        grid=(M // bm, nj),
        in_specs=[pl.BlockSpec((bm, K), lambda i, j: (i, 0)),
                  pl.BlockSpec((1, K, bn), lambda i, j: (layer, 0, j)),
                  pl.BlockSpec((1, K, bn), lambda i, j: (layer, 0, j)),
                  pl.BlockSpec((1, rb, ND), lambda i, j: (layer, cast_blk(i, j), 0))],
        out_specs=[pl.BlockSpec((bm, bn), lambda i, j: (i, j)),
                   pl.BlockSpec((rb, ND), lambda i, j: (cast_blk(i, j), 0))],
        out_shape=[jax.ShapeDtypeStruct((M, N), BF16),
                   jax.ShapeDtypeStruct((KD, ND), BF16)],
        compiler_params=_params(2, 60),
        name="swiglu_up",
    )(a, wg, wu, wd)


def _merge_kernel(af_ref, ap_ref, aoc_ref, aol_ref, wf_ref, wp_ref, wo_ref, g0_ref, g1_ref, g2_ref, o_ref, *, n_a):
    first = pl.program_id(1) < n_a
    for rows in _slabs(af_ref.shape[0]):
        ao = jnp.where(first, aoc_ref[rows, :], aol_ref[rows, :])
        y = g0_ref[rows, :].astype(F32) * _dot(af_ref[rows, :], wf_ref[0])
        y = y + g1_ref[rows, :].astype(F32) * _dot(ap_ref[rows, :], wp_ref[0])
        y = y + g2_ref[rows, :].astype(F32) * _dot(ao, wo_ref[0])
        o_ref[rows, :] = y.astype(o_ref.dtype)


def _merge(af, ap, aos, wf, wp, wo, gates, layer, bm, bn):
    M = af.shape[0]
    N = wf.shape[2]
    bm = min(bm, aos[0].shape[0], aos[1].shape[0])
    nb = N // bn
    a_spec = lambda a: pl.BlockSpec((bm, a.shape[1]), lambda j, i: (i, 0))
    w_spec = lambda w: pl.BlockSpec((1, w.shape[1], bn), lambda j, i: (layer, 0, j))
    g_spec = lambda k: pl.BlockSpec((bm, bn), lambda j, i: (i, k * nb + j))
    return pl.pallas_call(
        functools.partial(_merge_kernel, n_a=aos[0].shape[0] // bm),
        grid=(nb, M // bm),
        in_specs=[a_spec(af), a_spec(ap)] + _row_specs(aos, bm, aos[0].shape[1], 1) + [
                  w_spec(wf), w_spec(wp), w_spec(wo), g_spec(0), g_spec(1), g_spec(2)],
        out_specs=pl.BlockSpec((bm, bn), lambda j, i: (i, j)),
        out_shape=jax.ShapeDtypeStruct((M, N), BF16),
        compiler_params=_params(2, 60),
        name="merge",
    )(af, ap, *aos, wf, wp, wo, gates, gates, gates)


def _dft_kernel(f_ref, ch_ref, pc_ref, pl_ref, o_ref, *, n_ctx_chunks, s_ctx, fg):
    i = pl.program_id(0)
    tch = f_ref.shape[0]

    def mix(pos_ref, s):
        scale = 1.0 / float(np.sqrt(s * fg))
        for g in range(N_FOURIER_GROUPS):
            x = f_ref[:, g * fg:(g + 1) * fg].astype(BF16)
            xcs = _dot(x, ch_ref[...]).astype(BF16)
            for q in range(tch // s):
                rows = slice(q * s, (q + 1) * s)
                stacked = jnp.concatenate([xcs[rows, :fg], xcs[rows, fg:]], axis=0)
                y = _dot(pos_ref[...], stacked) * scale
                o_ref[rows, g * fg:(g + 1) * fg] = y.astype(o_ref.dtype)

    @pl.when(i < n_ctx_chunks)
    def _():
        mix(pc_ref, s_ctx)

    @pl.when(i >= n_ctx_chunks)
    def _():
        mix(pl_ref, tch)


def _dft_tables(s_ctx, s_lat, fg):
    def cs(n):
        k = np.arange(n, dtype=np.int64)
        ang = 2.0 * np.pi * ((k[:, None] * k[None, :]) % n).astype(np.float64) / n
        return np.cos(ang), np.sin(ang)
    cc, sc = cs(fg)
    chan = np.concatenate([cc, sc], axis=1)
    pos = []
    for s in (s_ctx, s_lat):
        c, sn = cs(s)
        pos.append(np.concatenate([c, -sn], axis=1))
    return (jnp.asarray(chan, BF16), jnp.asarray(pos[0], BF16), jnp.asarray(pos[1], BF16))


def _dft(z, dims, df):
    M = z.shape[0]
    tch, s_ctx = dims["tch"], dims["s_ctx"]
    fg = df // N_FOURIER_GROUPS
    chan, pos_c, pos_l = _dft_tables(s_ctx, tch, fg)
    full = lambda a: pl.BlockSpec(a.shape, lambda i: (0, 0))
    return pl.pallas_call(
        functools.partial(_dft_kernel, n_ctx_chunks=dims["m_ctx"] // tch, s_ctx=s_ctx, fg=fg),
        grid=(M // tch,),
        in_specs=[pl.BlockSpec((tch, df), lambda i: (i, 0)), full(chan), full(pos_c), full(pos_l)],
        out_specs=pl.BlockSpec((tch, df), lambda i: (i, 0)),
        out_shape=jax.ShapeDtypeStruct((M, df), BF16),
        compiler_params=_params(1, 48),
        name="dft_mix",
    )(z, chan, pos_c, pos_l)


def _pool_kernel(p_ref, ps_ref, o_ref, *, n_ctx_chunks, s_ctx, pg):
    i = pl.program_id(0)
    tch = p_ref.shape[0]
    s_seq = jnp.where(i < n_ctx_chunks, s_ctx, tch)
    pos = jax.lax.broadcasted_iota(jnp.int32, (tch, pg), 0) & (s_seq - 1)
    for gi, w in enumerate(POOL_WINDOWS):
        half = w // 2
        cols = slice(gi * pg, (gi + 1) * pg)
        x = p_ref[:, cols]
        tot = jnp.zeros_like(x)
        for d in range(-half, w - half):
            shifted = x if d == 0 else pltpu.roll(x, (-d) % tch, 0)
            ok = (pos + d >= 0) & (pos + d < s_seq)
            tot = tot + jnp.where(ok, shifted, 0.0)
        cnt = (jnp.minimum(pos + (w - half), s_seq) - jnp.maximum(pos - half, 0)).astype(F32)
        o_ref[:, cols] = ((tot / cnt - x) * ps_ref[:, cols]).astype(o_ref.dtype)


def _pool(z, pool_scale_l, dims, dp, col_blk):
    M = z.shape[0]
    tch = dims["tch"]
    return pl.pallas_call(
        functools.partial(_pool_kernel, n_ctx_chunks=dims["m_ctx"] // tch, s_ctx=dims["s_ctx"],
                          pg=dp // len(POOL_WINDOWS)),
        grid=(M // tch,),
        in_specs=[pl.BlockSpec((tch, dp), lambda i: (i, col_blk)),
                  pl.BlockSpec((1, dp), lambda i: (0, 0))],
        out_specs=pl.BlockSpec((tch, dp), lambda i: (i, 0)),
        out_shape=jax.ShapeDtypeStruct((M, dp), BF16),
        compiler_params=_params(1, 32),
        name="pool",
    )(z, pool_scale_l.reshape(1, dp))


def _rope_tables(n_tokens):
    rows = n_tokens // GRID_W
    row = np.repeat(np.arange(rows, dtype=np.float64), GRID_W)
    col = np.tile(np.arange(GRID_W, dtype=np.float64), rows)
    inv = ROPE_BASE ** (-np.arange(ROPE_DIM // 4, dtype=np.float64) * (4.0 / ROPE_DIM))
    ar = row[:, None] * inv[None, :]
    ac = col[:, None] * inv[None, :]
    ang = np.concatenate([ar, ar, ac, ac], axis=1)
    sign = np.tile(np.concatenate([-np.ones(ROPE_DIM // 4), np.ones(ROPE_DIM // 4)]), 2)
    cos = np.concatenate([np.cos(ang), np.ones_like(ang)], axis=1)
    sin = np.concatenate([np.sin(ang) * sign[None, :], np.zeros_like(ang)], axis=1)
    return jnp.asarray(cos, F32), jnp.asarray(sin, F32)


def _rope(x, cos, sin):
    q = ROPE_DIM // 4
    lane = jax.lax.broadcasted_iota(jnp.int32, x.shape, 1)
    first = (lane & (2 * q - 1)) < q
    swapped = jnp.where(first, pltpu.roll(x, LANE - q, 1), pltpu.roll(x, q, 1))
    return x * cos + swapped * sin


def _q_kernel(cq_ref, g_ref, w_ref, cos_ref, sin_ref, o_ref, wp_ref, *, n_ctx_tiles):
    i = pl.program_id(0)

    @pl.when(i == 0)
    def _():
        hd = NOPE_DIM + ROPE_DIM
        for h in range(N_HEADS):
            wp_ref[:, h * HEAD_PAD:h * HEAD_PAD + hd] = w_ref[0, :, h * hd:(h + 1) * hd].astype(BF16)
            wp_ref[:, h * HEAD_PAD + hd:(h + 1) * HEAD_PAD] = jnp.zeros((wp_ref.shape[0], HEAD_PAD - hd), BF16)

    x = cq_ref[...]
    r = jax.lax.rsqrt(jnp.mean(x * x, axis=-1, keepdims=True) + EPS)
    q = _dot((x * r * g_ref[...]).astype(BF16), wp_ref[...]) * Q_SCALE

    @pl.when(i < n_ctx_tiles)
    def _():
        o_ref[...] = q.astype(o_ref.dtype)

    @pl.when(i >= n_ctx_tiles)
    def _():
        cos, sin = cos_ref[...], sin_ref[...]
        for h in range(N_HEADS):
            lo = h * HEAD_PAD
            o_ref[:, lo:lo + NOPE_DIM] = q[:, lo:lo + NOPE_DIM].astype(o_ref.dtype)
            pe = q[:, lo + NOPE_DIM:lo + HEAD_PAD]
            o_ref[:, lo + NOPE_DIM:lo + HEAD_PAD] = _rope(pe, cos, sin).astype(o_ref.dtype)


def _q_proj(z, g_q_l, w_q_b, layer, cos, sin, dims, ql, col_blk):
    M = z.shape[0]
    N = N_HEADS * HEAD_PAD
    bm = min(512, dims["tch"])
    n_ctx_tiles = dims["m_ctx"] // bm
    per_seq = dims["tch"] // bm
    pos_blk = lambda i: (jnp.maximum(i - n_ctx_tiles, 0) % per_seq, 0)
    return pl.pallas_call(
        functools.partial(_q_kernel, n_ctx_tiles=n_ctx_tiles),
        grid=(M // bm,),
        in_specs=[pl.BlockSpec((bm, ql), lambda i: (i, col_blk)),
                  pl.BlockSpec((1, ql), lambda i: (0, 0)),
                  pl.BlockSpec((1,) + w_q_b.shape[1:], lambda i: (layer, 0, 0), pipeline_mode=pl.Buffered(1)),
                  pl.BlockSpec((bm, LANE), pos_blk),
                  pl.BlockSpec((bm, LANE), pos_blk)],
        out_specs=pl.BlockSpec((bm, N), lambda i: (i, 0)),
        out_shape=jax.ShapeDtypeStruct((M, N), BF16),
        scratch_shapes=[pltpu.VMEM((ql, N), BF16)],
        compiler_params=_params(1, 56),
        name="q_proj",
    )(z, g_q_l.reshape(1, ql), w_q_b, cos, sin)


def _kv_kernel(ckv_ref, g_ref, w_ref, kpe_ref, cos_ref, sin_ref, ckv_o, kv_o, kpe_o, *, n_ctx_tiles):
    i = pl.program_id(0)
    x = ckv_ref[...]
    r = jax.lax.rsqrt(jnp.mean(x * x, axis=-1, keepdims=True) + EPS)
    c = x * r * g_ref[...]
    ckv_o[...] = c
    kv_o[...] = _dot(c.astype(BF16), w_ref[...]).astype(kv_o.dtype)
    kpe = kpe_ref[...]
    kpe = jnp.where(jax.lax.broadcasted_iota(jnp.int32, kpe.shape, 1) < ROPE_DIM, kpe, 0.0)

    @pl.when(i < n_ctx_tiles)
    def _():
        kpe_o[...] = kpe.astype(kpe_o.dtype)

    @pl.when(i >= n_ctx_tiles)
    def _():
        kpe_o[...] = _rope(kpe, cos_ref[...], sin_ref[...]).astype(kpe_o.dtype)


def _kv_proj(z, kpe_raw, g_kv_l, w_kv, cos, sin, dims, kvl, ckv_blk):
    M = z.shape[0]
    N = w_kv.shape[1]
    bm = min(512, dims["tch"])
    n_ctx_tiles = dims["m_ctx"] // bm
    per_seq = dims["tch"] // bm
    pos_blk = lambda i: (jnp.maximum(i - n_ctx_tiles, 0) % per_seq, 0)
    return pl.pallas_call(
        functools.partial(_kv_kernel, n_ctx_tiles=n_ctx_tiles),
        grid=(M // bm,),
        in_specs=[pl.BlockSpec((bm, kvl), lambda i: (i, ckv_blk)),
                  pl.BlockSpec((1, kvl), lambda i: (0, 0)),
                  pl.BlockSpec((kvl, N), lambda i: (0, 0)),
                  pl.BlockSpec((bm, LANE), lambda i: (i, 0)),
                  pl.BlockSpec((bm, LANE), pos_blk),
                  pl.BlockSpec((bm, LANE), pos_blk)],
        out_specs=[pl.BlockSpec((bm, kvl), lambda i: (i, 0)),
                   pl.BlockSpec((bm, N), lambda i: (i, 0)),
                   pl.BlockSpec((bm, LANE), lambda i: (i, 0))],
        out_shape=[jax.ShapeDtypeStruct((M, kvl), F32),
                   jax.ShapeDtypeStruct((M, N), BF16),
                   jax.ShapeDtypeStruct((M, LANE), BF16)],
        compiler_params=_params(1, 48),
        name="kv_proj",
    )(z, g_kv_l.reshape(1, kvl), w_kv, kpe_raw, cos, sin)


def _kv_cache_kernel(c_ref, w_ref, o_ref):
    o_ref[...] = _dot(c_ref[0, 0].astype(BF16), w_ref[...]).astype(o_ref.dtype)


def _kv_cache_proj(cache_ckv, w_kv, layer):
    DB, _, P, kvl = cache_ckv.shape
    N = w_kv.shape[1]
    return pl.pallas_call(
        _kv_cache_kernel,
        grid=(DB,),
        in_specs=[pl.BlockSpec((1, 1, P, kvl), lambda b: (b, layer, 0, 0)),
                  pl.BlockSpec((kvl, N), lambda b: (0, 0))],
        out_specs=pl.BlockSpec((P, N), lambda b: (b, 0)),
        out_shape=jax.ShapeDtypeStruct((DB * P, N), BF16),
        compiler_params=_params(1, 32),
        name="kv_cache_proj",
    )(cache_ckv, w_kv)


def _softmax_parts(parts):
    m = functools.reduce(jnp.maximum, [jnp.max(s, axis=-1, keepdims=True) for s in parts])
    es = [jnp.exp2(s - m) for s in parts]
    tot = functools.reduce(jnp.add, [jnp.sum(e, axis=-1, keepdims=True) for e in es])
    return [e.astype(BF16) for e in es], 1.0 / tot


def _attn_ctx_kernel(q_ref, kv_ref, kpe_ref, o_ref):
    kpe = kpe_ref[...]
    for h in range(N_HEADS):
        lo = h * HEAD_PAD
        k = jnp.concatenate([kv_ref[:, lo:lo + NOPE_DIM], kpe], axis=1)
        s = _dot_nt(q_ref[:, lo:lo + HEAD_PAD], k)
        (p,), inv = _softmax_parts([s])
        o = _dot(p, kv_ref[:, lo + NOPE_DIM:lo + HEAD_PAD]) * inv
        o_ref[:, h * V_DIM:(h + 1) * V_DIM] = o.astype(o_ref.dtype)


def _attn_ctx(q, kv, kpe, dims):
    M = q.shape[0]
    s = dims["s_ctx"]
    n = dims["m_ctx"] // s
    dv = N_HEADS * V_DIM
    return pl.pallas_call(
        _attn_ctx_kernel,
        grid=(n,),
        in_specs=[pl.BlockSpec((s, q.shape[1]), lambda b: (b, 0)),
                  pl.BlockSpec((s, kv.shape[1]), lambda b: (b, 0)),
                  pl.BlockSpec((s, LANE), lambda b: (b, 0))],
        out_specs=pl.BlockSpec((s, dv), lambda b: (b, 0)),
        out_shape=jax.ShapeDtypeStruct((dims["m_ctx"], dv), BF16),
        compiler_params=_params(1, 32),
        name="attn_ctx",
    )(q, kv, kpe)


def _attn_lat_kernel(q_ref, kv_ref, kpe_ref, kvc_ref, kpec_ref, o_ref):
    kpe = kpe_ref[...]
    kpec = kpec_ref[0, 0].astype(BF16)
    kpec = jnp.concatenate([kpec, jnp.zeros_like(kpec)], axis=1)
    for h in range(N_HEADS):
        lo = h * HEAD_PAD
        qh = q_ref[:, lo:lo + HEAD_PAD]
        k_own = jnp.concatenate([kv_ref[:, lo:lo + NOPE_DIM], kpe], axis=1)
        k_c = jnp.concatenate([kvc_ref[:, lo:lo + NOPE_DIM], kpec], axis=1)
        (p_c, p_o), inv = _softmax_parts([_dot_nt(qh, k_c), _dot_nt(qh, k_own)])
        o = _dot(p_c, kvc_ref[:, lo + NOPE_DIM:lo + HEAD_PAD]) + _dot(p_o, kv_ref[:, lo + NOPE_DIM:lo + HEAD_PAD])
        o_ref[:, h * V_DIM:(h + 1) * V_DIM] = (o * inv).astype(o_ref.dtype)


def _attn_lat(q, kv, kpe, kv_cache, cache_krope, layer, dims):
    tch, m_ctx = dims["tch"], dims["m_ctx"]
    DB, _, P, rd = cache_krope.shape
    bq = min(256, tch)
    nq = tch // bq
    dv = N_HEADS * V_DIM
    q_blk = lambda b, t: (m_ctx // bq + b * nq + t, 0)
    own_blk = lambda b, t: (m_ctx // tch + b, 0)
    return pl.pallas_call(
        _attn_lat_kernel,
        grid=(DB, nq),
        in_specs=[pl.BlockSpec((bq, q.shape[1]), q_blk),
                  pl.BlockSpec((tch, kv.shape[1]), own_blk),
                  pl.BlockSpec((tch, LANE), own_blk),
                  pl.BlockSpec((P, kv_cache.shape[1]), lambda b, t: (b, 0)),
                  pl.BlockSpec((1, 1, P, rd), lambda b, t: (b, layer, 0, 0))],
        out_specs=pl.BlockSpec((bq, dv), lambda b, t: (b * nq + t, 0)),
        out_shape=jax.ShapeDtypeStruct((DB * tch, dv), BF16),
        compiler_params=_params(2, 48),
        name="attn_lat",
    )(q, kv, kpe, kv_cache, cache_krope)


def kernel(x_prompt, x_sample, cache_ckv, cache_krope, c, c_ctx, w_ada, b_ada, g_norm1, g_norm2, w_in, w_fourier,
           pool_scale, w_pool, g_q, w_q_b, g_kv, w_kv_b, w_o_mla, w_out, w_gate, w_up, w_down, g_final):
    B, S, D = x_prompt.shape
    DB, DS, _ = x_sample.shape
    L = w_ada.shape[0]
    DF = w_fourier.shape[1]
    DP = pool_scale.shape[1]
    QL = g_q.shape[1]
    KVL = g_kv.shape[1]
    m_ctx, m_lat = B * S, DB * DS
    M = m_ctx + m_lat
    assert DS % S == 0 and m_ctx % DS == 0 and S & (S - 1) == 0 and DS & (DS - 1) == 0
    assert 1 + DB <= MOD_ROWS and DS % GRID_W == 0
    dims = dict(m_ctx=m_ctx, tch=DS, s_ctx=S)

    off_p, off_q, off_kv, off_pe = DF, DF + DP, DF + DP + QL, DF + DP + QL + KVL
    assert off_p % DP == 0 and off_q % QL == 0 and off_kv % KVL == 0 and off_pe % IN_BN == 0

    xs = (x_prompt.reshape(m_ctx, D), x_sample.reshape(m_lat, D))
    cond = jnp.zeros((MOD_ROWS, D), F32).at[0].set(c_ctx).at[1:1 + DB].set(c)
    b_ada3 = b_ada.reshape(L, 1, 6 * D)
    mod4 = _ada_mod(cond, w_ada, b_ada3, 0).reshape(1, MOD_ROWS, 1, 6 * D)
    cos, sin = _rope_tables(DS)
    w_in_t = jnp.swapaxes(w_in, 1, 2)
    wf_b, wp_b, wo_b = w_fourier.astype(BF16), w_pool.reshape(L, DP, D).astype(BF16), w_o_mla.astype(BF16)
    w_out_b = w_out.astype(BF16)

    ckv_layers, kpe_layers = [], []
    for l in range(L):
        w_kv = w_kv_b[l].astype(BF16)

        h = _norm_mod(xs, g_norm1[l:l + 1], mod4, 0, dims)
        z, kpe_raw, gates = _in_proj(h, w_in_t, l, off_pe, off_pe + ROPE_DIM, 2048)

        mixed = _dft(z, dims, DF)
        pooled = _pool(z, pool_scale[l], dims, DP, off_p // DP)
        q = _q_proj(z, g_q[l], w_q_b, l, cos, sin, dims, QL, off_q // QL)
        c_kv, kv, kpe = _kv_proj(z, kpe_raw, g_kv[l], w_kv, cos, sin, dims, KVL, off_kv // KVL)
        kv_cache = _kv_cache_proj(cache_ckv, w_kv, l)
        o_ctx = _attn_ctx(q, kv, kpe, dims)
        o_lat = _attn_lat(q, kv, kpe, kv_cache, cache_krope, l, dims)

        merged = _merge(mixed, pooled, (o_ctx, o_lat), wf_b, wp_b, wo_b, gates, l, 1024, 1024)
        x = _mm_resid(merged, w_out_b, xs, mod4, l, 2, 1024, 1024, dims, 60, "out_proj")

        h = _norm_mod((x,), g_norm2[l:l + 1], mod4, 3, dims)
        act, w_down_b = _swiglu_up(h, w_gate, w_up, w_down, l, 2048, 256)
        if l + 1 < L:
            x, mod_next = _mm_resid(act, w_down_b[None], (x,), mod4, 0, 5, 512, 512, dims, 60, "down_proj",
                                    ada_side=(cond, w_ada, b_ada3, l + 1))
            mod4 = mod_next.reshape(1, MOD_ROWS, 1, 6 * D)
        else:
            x = _mm_resid(act, w_down_b[None], (x,), mod4, 0, 5, 512, 512, dims, 56, "down_proj")
        xs = (x,)

        ckv_layers.append(c_kv[:m_ctx].reshape(B, S, KVL))
        kpe_layers.append(kpe_raw[:m_ctx, :ROPE_DIM].reshape(B, S, ROPE_DIM))

    y_prompt = _final_norm(x, g_final, 0, m_ctx).reshape(B, S, D)
    y_sample = _final_norm(x, g_final, m_ctx, m_lat).reshape(DB, DS, D)
    return (y_prompt, y_sample, jnp.stack(ckv_layers, axis=1), jnp.stack(kpe_layers, axis=1))
```

```python
import functools

import numpy as np
import jax
import jax.numpy as jnp
from jax.experimental import pallas as pl
from jax.experimental.pallas import tpu as pltpu

N_HEADS = 16
NOPE_DIM = 128
ROPE_DIM = 64
V_DIM = 128
HEAD_PAD = 256
GRID_W = 64
EPS = 1e-6
ROPE_BASE = 10000.0
SM_SCALE = (NOPE_DIM + ROPE_DIM) ** -0.5
Q_SCALE = SM_SCALE * float(np.log2(np.e))
POOL_WINDOWS = (2, 4, 8, 16)
N_FOURIER_GROUPS = 4
MOD_ROWS = 8
LANE = 128

BF16 = jnp.bfloat16
F32 = jnp.float32


def _params(n_grid, vmem_mb):
    return pltpu.CompilerParams(dimension_semantics=("arbitrary",) * n_grid,
                                vmem_limit_bytes=vmem_mb * 2 ** 20)


def _dot(a, b):
    return jnp.dot(a, b, preferred_element_type=F32)


def _dot_nt(a, b):
    return jax.lax.dot_general(a, b, (((1,), (1,)), ((), ())), preferred_element_type=F32)


SLAB_ROWS = 256


def _slabs(n_rows, slab_rows=SLAB_ROWS):
    r = int(np.gcd(slab_rows, n_rows))
    return [slice(s * r, (s + 1) * r) for s in range(n_rows // r)]


def _mod_row(i, bm, m_ctx, tch):
    start = i * bm
    return jnp.where(start < m_ctx, 0, 1 + (start - m_ctx) // tch)


def _ada_kernel(c_ref, w_ref, b_ref, o_ref):
    c = c_ref[...]
    a = (c * jax.nn.sigmoid(c)).astype(BF16)
    o_ref[0] = _dot(a, w_ref[0].astype(BF16)) + b_ref[0]


def _ada_mod(cond, w_ada, b_ada3, layer):
    _, D, N = w_ada.shape
    bn = 512
    return pl.pallas_call(
        _ada_kernel,
        grid=(N // bn,),
        in_specs=[pl.BlockSpec((MOD_ROWS, D), lambda j: (0, 0)),
                  pl.BlockSpec((1, D, bn), lambda j: (layer, 0, j)),
                  pl.BlockSpec((1, 1, bn), lambda j: (layer, 0, j))],
        out_specs=pl.BlockSpec((1, MOD_ROWS, bn), lambda j: (0, 0, j)),
        out_shape=jax.ShapeDtypeStruct((1, MOD_ROWS, N), F32),
        compiler_params=_params(1, 40),
        name="ada_mod",
    )(cond, w_ada, b_ada3)


def _ada_side_specs(cond, w_ada, b_ada3, layer, n_steps, step_of):
    _, D, N = w_ada.shape
    bn = min(t for t in range(LANE, N + 1, LANE) if N % t == 0 and N // t <= n_steps)
    n_side = N // bn
    blk = lambda *g: jnp.minimum(step_of(*g), n_side - 1)
    in_specs = [pl.BlockSpec((MOD_ROWS, D), lambda *g: (0, 0)),
                pl.BlockSpec((1, D, bn), lambda *g: (layer, 0, blk(*g))),
                pl.BlockSpec((1, 1, bn), lambda *g: (layer, 0, blk(*g)))]
    out_spec = pl.BlockSpec((1, MOD_ROWS, bn), lambda *g: (0, 0, blk(*g)))
    out_shape = jax.ShapeDtypeStruct((1, MOD_ROWS, N), F32)
    return n_side, in_specs, out_spec, out_shape


def _row_specs(xs, bm, bn, n_grid):
    idx = (lambda *g: (g[-1], g[0])) if n_grid == 2 else (lambda *g: (g[-1], 0))
    if len(xs) == 1:
        return [pl.BlockSpec((bm, bn), idx)]
    n_a = xs[0].shape[0] // bm
    last_b = xs[1].shape[0] // bm - 1
    spec_a = pl.BlockSpec((bm, bn), lambda *g: (jnp.minimum(idx(*g)[0], n_a - 1), idx(*g)[1]))
    spec_b = pl.BlockSpec((bm, bn), lambda *g: (jnp.clip(idx(*g)[0] - n_a, 0, last_b), idx(*g)[1]))
    return [spec_a, spec_b]


def _norm_mod_kernel(*refs, n_x, n_a):
    x_refs, (g_ref, sh_ref, sc_ref, o_ref) = refs[:n_x], refs[n_x:]
    x = x_refs[0][...] if n_x == 1 else jnp.where(pl.program_id(0) < n_a, x_refs[0][...], x_refs[1][...])
    r = jax.lax.rsqrt(jnp.mean(x * x, axis=-1, keepdims=True) + EPS)
    y = x * r * g_ref[...]
    o_ref[...] = (y * (1.0 + sc_ref[0, 0]) + sh_ref[0, 0]).astype(o_ref.dtype)


def _norm_mod(xs, g, mod4, shift_blk, dims):
    M = sum(x.shape[0] for x in xs)
    D = xs[0].shape[1]
    bm = min(512, dims["tch"])
    row = functools.partial(_mod_row, bm=bm, m_ctx=dims["m_ctx"], tch=dims["tch"])
    return pl.pallas_call(
        functools.partial(_norm_mod_kernel, n_x=len(xs), n_a=xs[0].shape[0] // bm),
        grid=(M // bm,),
        in_specs=_row_specs(xs, bm, D, 1) + [
            pl.BlockSpec((1, D), lambda i: (0, 0)),
            pl.BlockSpec((1, 1, 1, D), lambda i: (0, row(i), 0, shift_blk)),
            pl.BlockSpec((1, 1, 1, D), lambda i: (0, row(i), 0, shift_blk + 1))],
        out_specs=pl.BlockSpec((bm, D), lambda i: (i, 0)),
        out_shape=jax.ShapeDtypeStruct((M, D), BF16),
        compiler_params=_params(1, 56),
        name="norm_mod",
    )(*xs, g, mod4, mod4)


def _final_norm_kernel(x_ref, g_ref, o_ref):
    x = x_ref[...]
    r = jax.lax.rsqrt(jnp.mean(x * x, axis=-1, keepdims=True) + EPS)
    o_ref[...] = x * r * g_ref[...]


def _final_norm(x, g, row0, rows):
    D = x.shape[1]
    bm = int(np.gcd(512, np.gcd(rows, row0 if row0 else rows)))
    blk0 = row0 // bm
    return pl.pallas_call(
        _final_norm_kernel,
        grid=(rows // bm,),
        in_specs=[pl.BlockSpec((bm, D), lambda i: (blk0 + i, 0)),
                  pl.BlockSpec((1, D), lambda i: (0, 0))],
        out_specs=pl.BlockSpec((bm, D), lambda i: (i, 0)),
        out_shape=jax.ShapeDtypeStruct((rows, D), F32),
        compiler_params=_params(1, 48),
        name="final_norm",
    )(x, g.reshape(1, D))


IN_BN = 512


def _in_proj_kernel(h_ref, w_ref, wk_ref, z_ref, kpe_ref, g_ref, *, n_small_blk):
    j = pl.program_id(1)

    @pl.when(j < n_small_blk)
    def _():
        w = w_ref[0].astype(BF16)
        for rows in _slabs(h_ref.shape[0]):
            z_ref[rows, :] = _dot_nt(h_ref[rows, :], w)

    @pl.when(j == n_small_blk)
    def _():
        kpe_ref[...] = _dot_nt(h_ref[...], wk_ref[0].astype(BF16))

    @pl.when(j >= n_small_blk)
    def _():
        w = w_ref[0].astype(BF16)
        for rows in _slabs(h_ref.shape[0]):
            g_ref[rows, :] = jax.nn.sigmoid(_dot_nt(h_ref[rows, :], w)).astype(g_ref.dtype)


def _in_proj(h, w_t, layer, off_pe, n_small, bm):
    M, K = h.shape
    N = w_t.shape[1]
    bm = min(bm, M)
    n_small_blk = off_pe // IN_BN
    n_gate_blk = (N - n_small) // IN_BN
    assert off_pe % IN_BN == 0 and (N - n_small) % IN_BN == 0 and n_small % 8 == 0 and off_pe % LANE == 0

    def w_row(j):
        row = jnp.where(j < n_small_blk, j * IN_BN, n_small + (j - n_small_blk) * IN_BN)
        return pl.multiple_of(row, 8)

    return pl.pallas_call(
        functools.partial(_in_proj_kernel, n_small_blk=n_small_blk),
        grid=(M // bm, n_small_blk + n_gate_blk),
        in_specs=[pl.BlockSpec((bm, K), lambda i, j: (i, 0), pipeline_mode=pl.Buffered(1)),
                  pl.BlockSpec((pl.Element(1), pl.Element(IN_BN), pl.Element(K)), lambda i, j: (layer, w_row(j), 0)),
                  pl.BlockSpec((1, LANE, K), lambda i, j: (layer, off_pe // LANE, 0), pipeline_mode=pl.Buffered(1))],
        out_specs=[pl.BlockSpec((bm, IN_BN), lambda i, j: (i, jnp.minimum(j, n_small_blk - 1))),
                   pl.BlockSpec((bm, LANE), lambda i, j: (i, 0)),
                   pl.BlockSpec((bm, IN_BN), lambda i, j: (i, jnp.maximum(j - n_small_blk, 0)))],
        out_shape=[jax.ShapeDtypeStruct((M, off_pe), F32),
                   jax.ShapeDtypeStruct((M, LANE), F32),
                   jax.ShapeDtypeStruct((M, n_gate_blk * IN_BN), BF16)],
        compiler_params=_params(2, 60),
        name="in_proj",
    )(h, w_t, w_t)


def _mm_resid_kernel(a_ref, w_ref, *refs, n_x, n_a, n_side):
    x_refs, rest = refs[:n_x], refs[n_x:]
    gate_ref, o_ref = rest[0], rest[-2 if n_side else -1]
    first = pl.program_id(1) < n_a
    for rows in _slabs(a_ref.shape[0]):
        x = x_refs[0][rows, :] if n_x == 1 else jnp.where(first, x_refs[0][rows, :], x_refs[1][rows, :])
        o_ref[rows, :] = x + gate_ref[0, 0] * _dot(a_ref[rows, :], w_ref[0])

    if n_side:
        c_ref, wa_ref, ba_ref, mod_o = rest[1], rest[2], rest[3], rest[-1]

        @pl.when(pl.program_id(0) * pl.num_programs(1) + pl.program_id(1) < n_side)
        def _():
            _ada_kernel(c_ref, wa_ref, ba_ref, mod_o)


def _mm_resid(a, w, xs, mod4, layer, gate_blk, bm, bn, dims, vmem_mb, name, ada_side=None):
    M, K = a.shape
    N = w.shape[2]
    bm = min(bm, dims["tch"])
    row = functools.partial(_mod_row, bm=bm, m_ctx=dims["m_ctx"], tch=dims["tch"])
    nb, ni = N // bn, M // bm
    in_specs = [pl.BlockSpec((bm, K), lambda j, i: (i, 0)),
                pl.BlockSpec((1, K, bn), lambda j, i: (layer, 0, j))] + _row_specs(xs, bm, bn, 2) + [
                pl.BlockSpec((1, 1, 1, bn), lambda j, i: (0, row(i), 0, gate_blk * nb + j))]
    out_specs = [pl.BlockSpec((bm, bn), lambda j, i: (i, j))]
    out_shape = [jax.ShapeDtypeStruct((M, N), F32)]
    args = [a, w, *xs, mod4]
    n_side = 0
    if ada_side is not None:
        n_side, side_in, side_out, side_shape = _ada_side_specs(*ada_side, nb * ni, lambda j, i: j * ni + i)
        in_specs += side_in
        out_specs.append(side_out)
        out_shape.append(side_shape)
        args += list(ada_side[:3])
    res = pl.pallas_call(
        functools.partial(_mm_resid_kernel, n_x=len(xs), n_a=xs[0].shape[0] // bm, n_side=n_side),
        grid=(nb, ni),
        in_specs=in_specs,
        out_specs=out_specs,
        out_shape=out_shape,
        compiler_params=_params(2, vmem_mb),
        name=name,
    )(*args)
    return res if n_side else res[0]


def _swiglu_kernel(a_ref, wg_ref, wu_ref, wd_ref, o_ref, wd_o, *, n_cast):
    wg = wg_ref[0].astype(BF16)
    wu = wu_ref[0].astype(BF16)
    for rows in _slabs(a_ref.shape[0]):
        a = a_ref[rows, :]
        g = _dot(a, wg)
        u = _dot(a, wu)
        o_ref[rows, :] = (g * jax.nn.sigmoid(g) * u).astype(o_ref.dtype)

    @pl.when(pl.program_id(0) * pl.num_programs(1) + pl.program_id(1) < n_cast)
    def _():
        wd_o[...] = wd_ref[0].astype(wd_o.dtype)


def _swiglu_up(a, wg, wu, wd, layer, bm, bn):
    M, K = a.shape
    N = wg.shape[2]
    KD, ND = wd.shape[1:]
    bm = min(bm, M)
    nj = N // bn
    n_steps = (M // bm) * nj
    n_cast = max(n for n in range(1, n_steps + 1) if KD % n == 0 and (KD // n) % 16 == 0)
    rb = KD // n_cast
    cast_blk = lambda i, j: jnp.minimum(i * nj + j, n_cast - 1)
    return pl.pallas_call(
        functools.partial(_swiglu_kernel, n_cast=n_cast),
        grid=(M // bm, nj),
        in_specs=[pl.BlockSpec((bm, K), lambda i, j: (i, 0)),
                  pl.BlockSpec((1, K, bn), lambda i, j: (layer, 0, j)),
                  pl.BlockSpec((1, K, bn), lambda i, j: (layer, 0, j)),
                  pl.BlockSpec((1, rb, ND), lambda i, j: (layer, cast_blk(i, j), 0))],
        out_specs=[pl.BlockSpec((bm, bn), lambda i, j: (i, j)),
                   pl.BlockSpec((rb, ND), lambda i, j: (cast_blk(i, j), 0))],
        out_shape=[jax.ShapeDtypeStruct((M, N), BF16),
                   jax.ShapeDtypeStruct((KD, ND), BF16)],
        compiler_params=_params(2, 60),
        name="swiglu_up",
    )(a, wg, wu, wd)


def _merge_kernel(af_ref, ap_ref, aoc_ref, aol_ref, wf_ref, wp_ref, wo_ref, g0_ref, g1_ref, g2_ref, o_ref, *, n_a):
    first = pl.program_id(1) < n_a
    for rows in _slabs(af_ref.shape[0]):
        ao = jnp.where(first, aoc_ref[rows, :], aol_ref[rows, :])
        y = g0_ref[rows, :].astype(F32) * _dot(af_ref[rows, :], wf_ref[0])
        y = y + g1_ref[rows, :].astype(F32) * _dot(ap_ref[rows, :], wp_ref[0])
        y = y + g2_ref[rows, :].astype(F32) * _dot(ao, wo_ref[0])
        o_ref[rows, :] = y.astype(o_ref.dtype)


def _merge(af, ap, aos, wf, wp, wo, gates, layer, bm, bn):
    M = af.shape[0]
    N = wf.shape[2]
    bm = min(bm, aos[0].shape[0], aos[1].shape[0])
    nb = N // bn
    a_spec = lambda a: pl.BlockSpec((bm, a.shape[1]), lambda j, i: (i, 0))
    w_spec = lambda w: pl.BlockSpec((1, w.shape[1], bn), lambda j, i: (layer, 0, j))
    g_spec = lambda k: pl.BlockSpec((bm, bn), lambda j, i: (i, k * nb + j))
    return pl.pallas_call(
        functools.partial(_merge_kernel, n_a=aos[0].shape[0] // bm),
        grid=(nb, M // bm),
        in_specs=[a_spec(af), a_spec(ap)] + _row_specs(aos, bm, aos[0].shape[1], 1) + [
                  w_spec(wf), w_spec(wp), w_spec(wo), g_spec(0), g_spec(1), g_spec(2)],
        out_specs=pl.BlockSpec((bm, bn), lambda j, i: (i, j)),
        out_shape=jax.ShapeDtypeStruct((M, N), BF16),
        compiler_params=_params(2, 60),
        name="merge",
    )(af, ap, *aos, wf, wp, wo, gates, gates, gates)


def _dft_kernel(f_ref, ch_ref, pc_ref, pl_ref, o_ref, *, n_ctx_chunks, s_ctx, fg):
    i = pl.program_id(0)
    tch = f_ref.shape[0]

    def mix(pos_ref, s):
        scale = 1.0 / float(np.sqrt(s * fg))
        for g in range(N_FOURIER_GROUPS):
            x = f_ref[:, g * fg:(g + 1) * fg].astype(BF16)
            xcs = _dot(x, ch_ref[...]).astype(BF16)
            for q in range(tch // s):
                rows = slice(q * s, (q + 1) * s)
                stacked = jnp.concatenate([xcs[rows, :fg], xcs[rows, fg:]], axis=0)
                y = _dot(pos_ref[...], stacked) * scale
                o_ref[rows, g * fg:(g + 1) * fg] = y.astype(o_ref.dtype)

    @pl.when(i < n_ctx_chunks)
    def _():
        mix(pc_ref, s_ctx)

    @pl.when(i >= n_ctx_chunks)
    def _():
        mix(pl_ref, tch)


def _dft_tables(s_ctx, s_lat, fg):
    def cs(n):
        k = np.arange(n, dtype=np.int64)
        ang = 2.0 * np.pi * ((k[:, None] * k[None, :]) % n).astype(np.float64) / n
        return np.cos(ang), np.sin(ang)
    cc, sc = cs(fg)
    chan = np.concatenate([cc, sc], axis=1)
    pos = []
    for s in (s_ctx, s_lat):
        c, sn = cs(s)
        pos.append(np.concatenate([c, -sn], axis=1))
    return (jnp.asarray(chan, BF16), jnp.asarray(pos[0], BF16), jnp.asarray(pos[1], BF16))


def _dft(z, dims, df):
    M = z.shape[0]
    tch, s_ctx = dims["tch"], dims["s_ctx"]
    fg = df // N_FOURIER_GROUPS
    chan, pos_c, pos_l = _dft_tables(s_ctx, tch, fg)
    full = lambda a: pl.BlockSpec(a.shape, lambda i: (0, 0))
    return pl.pallas_call(
        functools.partial(_dft_kernel, n_ctx_chunks=dims["m_ctx"] // tch, s_ctx=s_ctx, fg=fg),
        grid=(M // tch,),
        in_specs=[pl.BlockSpec((tch, df), lambda i: (i, 0)), full(chan), full(pos_c), full(pos_l)],
        out_specs=pl.BlockSpec((tch, df), lambda i: (i, 0)),
        out_shape=jax.ShapeDtypeStruct((M, df), BF16),
        compiler_params=_params(1, 48),
        name="dft_mix",
    )(z, chan, pos_c, pos_l)


def _pool_kernel(p_ref, ps_ref, o_ref, *, n_ctx_chunks, s_ctx, pg):
    i = pl.program_id(0)
    tch = p_ref.shape[0]
    s_seq = jnp.where(i < n_ctx_chunks, s_ctx, tch)
    pos = jax.lax.broadcasted_iota(jnp.int32, (tch, pg), 0) & (s_seq - 1)
    for gi, w in enumerate(POOL_WINDOWS):
        half = w // 2
        assert w == 2 * half and half & (half - 1) == 0
        cols = slice(gi * pg, (gi + 1) * pg)
        x = p_ref[:, cols]
        before = jnp.where(pos >= 1, pltpu.roll(x, 1, 0), 0.0)
        after = x
        m = 1
        while m < half:
            before = before + jnp.where(pos >= m, pltpu.roll(before, m, 0), 0.0)
            after = after + jnp.where(pos + m < s_seq, pltpu.roll(after, tch - m, 0), 0.0)
            m *= 2
        cnt = (jnp.minimum(pos + half, s_seq) - jnp.maximum(pos - half, 0)).astype(F32)
        o_ref[:, cols] = (((before + after) / cnt - x) * ps_ref[:, cols]).astype(o_ref.dtype)


def _pool(z, pool_scale_l, dims, dp, col_blk):
    M = z.shape[0]
    tch = dims["tch"]
    return pl.pallas_call(
        functools.partial(_pool_kernel, n_ctx_chunks=dims["m_ctx"] // tch, s_ctx=dims["s_ctx"],
                          pg=dp // len(POOL_WINDOWS)),
        grid=(M // tch,),
        in_specs=[pl.BlockSpec((tch, dp), lambda i: (i, col_blk)),
                  pl.BlockSpec((1, dp), lambda i: (0, 0))],
        out_specs=pl.BlockSpec((tch, dp), lambda i: (i, 0)),
        out_shape=jax.ShapeDtypeStruct((M, dp), BF16),
        compiler_params=_params(1, 32),
        name="pool",
    )(z, pool_scale_l.reshape(1, dp))


def _rope_tables(n_tokens):
    rows = n_tokens // GRID_W
    row = np.repeat(np.arange(rows, dtype=np.float64), GRID_W)
    col = np.tile(np.arange(GRID_W, dtype=np.float64), rows)
    inv = ROPE_BASE ** (-np.arange(ROPE_DIM // 4, dtype=np.float64) * (4.0 / ROPE_DIM))
    ar = row[:, None] * inv[None, :]
    ac = col[:, None] * inv[None, :]
    ang = np.concatenate([ar, ar, ac, ac], axis=1)
    sign = np.tile(np.concatenate([-np.ones(ROPE_DIM // 4), np.ones(ROPE_DIM // 4)]), 2)
    cos = np.concatenate([np.cos(ang), np.ones_like(ang)], axis=1)
    sin = np.concatenate([np.sin(ang) * sign[None, :], np.zeros_like(ang)], axis=1)
    return jnp.asarray(cos, F32), jnp.asarray(sin, F32)


def _rope(x, cos, sin):
    q = ROPE_DIM // 4
    lane = jax.lax.broadcasted_iota(jnp.int32, x.shape, 1)
    first = (lane & (2 * q - 1)) < q
    swapped = jnp.where(first, pltpu.roll(x, LANE - q, 1), pltpu.roll(x, q, 1))
    return x * cos + swapped * sin


def _q_kernel(cq_ref, g_ref, w_ref, cos_ref, sin_ref, o_ref, wp_ref, *, n_ctx_tiles):
    i = pl.program_id(0)

    @pl.when(i == 0)
    def _():
        hd = NOPE_DIM + ROPE_DIM
        for h in range(N_HEADS):
            wp_ref[:, h * HEAD_PAD:h * HEAD_PAD + hd] = w_ref[0, :, h * hd:(h + 1) * hd].astype(BF16)
            wp_ref[:, h * HEAD_PAD + hd:(h + 1) * HEAD_PAD] = jnp.zeros((wp_ref.shape[0], HEAD_PAD - hd), BF16)

    def q_rows(rows):
        x = cq_ref[rows, :]
        r = jax.lax.rsqrt(jnp.mean(x * x, axis=-1, keepdims=True) + EPS)
        return _dot((x * r * g_ref[...]).astype(BF16), wp_ref[...]) * Q_SCALE

    @pl.when(i < n_ctx_tiles)
    def _():
        for rows in _slabs(cq_ref.shape[0]):
            o_ref[rows, :] = q_rows(rows).astype(o_ref.dtype)

    @pl.when(i >= n_ctx_tiles)
    def _():
        for rows in _slabs(cq_ref.shape[0]):
            q = q_rows(rows)
            cos, sin = cos_ref[rows, :], sin_ref[rows, :]
            for h in range(N_HEADS):
                lo = h * HEAD_PAD
                o_ref[rows, lo:lo + NOPE_DIM] = q[:, lo:lo + NOPE_DIM].astype(o_ref.dtype)
                pe = q[:, lo + NOPE_DIM:lo + HEAD_PAD]
                o_ref[rows, lo + NOPE_DIM:lo + HEAD_PAD] = _rope(pe, cos, sin).astype(o_ref.dtype)


def _q_proj(z, g_q_l, w_q_b, layer, cos, sin, dims, ql, col_blk):
    M = z.shape[0]
    N = N_HEADS * HEAD_PAD
    bm = min(512, dims["tch"])
    n_ctx_tiles = dims["m_ctx"] // bm
    per_seq = dims["tch"] // bm
    pos_blk = lambda i: (jnp.maximum(i - n_ctx_tiles, 0) % per_seq, 0)
    return pl.pallas_call(
        functools.partial(_q_kernel, n_ctx_tiles=n_ctx_tiles),
        grid=(M // bm,),
        in_specs=[pl.BlockSpec((bm, ql), lambda i: (i, col_blk)),
                  pl.BlockSpec((1, ql), lambda i: (0, 0)),
                  pl.BlockSpec((1,) + w_q_b.shape[1:], lambda i: (layer, 0, 0), pipeline_mode=pl.Buffered(1)),
                  pl.BlockSpec((bm, LANE), pos_blk),
                  pl.BlockSpec((bm, LANE), pos_blk)],
        out_specs=pl.BlockSpec((bm, N), lambda i: (i, 0)),
        out_shape=jax.ShapeDtypeStruct((M, N), BF16),
        scratch_shapes=[pltpu.VMEM((ql, N), BF16)],
        compiler_params=_params(1, 56),
        name="q_proj",
    )(z, g_q_l.reshape(1, ql), w_q_b, cos, sin)


def _kv_kernel(ckv_ref, g_ref, w_ref, kpe_ref, cos_ref, sin_ref, ckv_o, kv_o, kpe_o, *, n_ctx_tiles):
    i = pl.program_id(0)
    x = ckv_ref[...]
    r = jax.lax.rsqrt(jnp.mean(x * x, axis=-1, keepdims=True) + EPS)
    c = x * r * g_ref[...]
    ckv_o[...] = c
    kv_o[...] = _dot(c.astype(BF16), w_ref[...]).astype(kv_o.dtype)
    kpe = kpe_ref[...]
    kpe = jnp.where(jax.lax.broadcasted_iota(jnp.int32, kpe.shape, 1) < ROPE_DIM, kpe, 0.0)

    @pl.when(i < n_ctx_tiles)
    def _():
        kpe_o[...] = kpe.astype(kpe_o.dtype)

    @pl.when(i >= n_ctx_tiles)
    def _():
        kpe_o[...] = _rope(kpe, cos_ref[...], sin_ref[...]).astype(kpe_o.dtype)


def _kv_proj(z, kpe_raw, g_kv_l, w_kv, cos, sin, dims, kvl, ckv_blk):
    M = z.shape[0]
    N = w_kv.shape[1]
    bm = min(512, dims["tch"])
    n_ctx_tiles = dims["m_ctx"] // bm
    per_seq = dims["tch"] // bm
    pos_blk = lambda i: (jnp.maximum(i - n_ctx_tiles, 0) % per_seq, 0)
    return pl.pallas_call(
        functools.partial(_kv_kernel, n_ctx_tiles=n_ctx_tiles),
        grid=(M // bm,),
        in_specs=[pl.BlockSpec((bm, kvl), lambda i: (i, ckv_blk)),
                  pl.BlockSpec((1, kvl), lambda i: (0, 0)),
                  pl.BlockSpec((kvl, N), lambda i: (0, 0)),
                  pl.BlockSpec((bm, LANE), lambda i: (i, 0)),
                  pl.BlockSpec((bm, LANE), pos_blk),
                  pl.BlockSpec((bm, LANE), pos_blk)],
        out_specs=[pl.BlockSpec((bm, kvl), lambda i: (i, 0)),
                   pl.BlockSpec((bm, N), lambda i: (i, 0)),
                   pl.BlockSpec((bm, LANE), lambda i: (i, 0))],
        out_shape=[jax.ShapeDtypeStruct((M, kvl), F32),
                   jax.ShapeDtypeStruct((M, N), BF16),
                   jax.ShapeDtypeStruct((M, LANE), BF16)],
        compiler_params=_params(1, 48),
        name="kv_proj",
    )(z, g_kv_l.reshape(1, kvl), w_kv, kpe_raw, cos, sin)


def _kv_cache_kernel(c_ref, w_ref, o_ref):
    o_ref[...] = _dot(c_ref[0, 0].astype(BF16), w_ref[...]).astype(o_ref.dtype)


def _kv_cache_proj(cache_ckv, w_kv, layer):
    DB, _, P, kvl = cache_ckv.shape
    N = w_kv.shape[1]
    return pl.pallas_call(
        _kv_cache_kernel,
        grid=(DB,),
        in_specs=[pl.BlockSpec((1, 1, P, kvl), lambda b: (b, layer, 0, 0)),
                  pl.BlockSpec((kvl, N), lambda b: (0, 0))],
        out_specs=pl.BlockSpec((P, N), lambda b: (b, 0)),
        out_shape=jax.ShapeDtypeStruct((DB * P, N), BF16),
        compiler_params=_params(1, 32),
        name="kv_cache_proj",
    )(cache_ckv, w_kv)


def _softmax_parts(parts):
    m = functools.reduce(jnp.maximum, [jnp.max(s, axis=-1, keepdims=True) for s in parts])
    es = [jnp.exp2(s - m) for s in parts]
    tot = functools.reduce(jnp.add, [jnp.sum(e, axis=-1, keepdims=True) for e in es])
    return [e.astype(BF16) for e in es], 1.0 / tot


def _attn_ctx_kernel(q_ref, kv_ref, kpe_ref, o_ref):
    kpe = kpe_ref[...]
    for h in range(N_HEADS):
        lo = h * HEAD_PAD
        k = jnp.concatenate([kv_ref[:, lo:lo + NOPE_DIM], kpe], axis=1)
        s = _dot_nt(q_ref[:, lo:lo + HEAD_PAD], k)
        (p,), inv = _softmax_parts([s])
        o = _dot(p, kv_ref[:, lo + NOPE_DIM:lo + HEAD_PAD]) * inv
        o_ref[:, h * V_DIM:(h + 1) * V_DIM] = o.astype(o_ref.dtype)


def _attn_ctx(q, kv, kpe, dims):
    M = q.shape[0]
    s = dims["s_ctx"]
    n = dims["m_ctx"] // s
    dv = N_HEADS * V_DIM
    return pl.pallas_call(
        _attn_ctx_kernel,
        grid=(n,),
        in_specs=[pl.BlockSpec((s, q.shape[1]), lambda b: (b, 0)),
                  pl.BlockSpec((s, kv.shape[1]), lambda b: (b, 0)),
                  pl.BlockSpec((s, LANE), lambda b: (b, 0))],
        out_specs=pl.BlockSpec((s, dv), lambda b: (b, 0)),
        out_shape=jax.ShapeDtypeStruct((dims["m_ctx"], dv), BF16),
        compiler_params=_params(1, 32),
        name="attn_ctx",
    )(q, kv, kpe)


def _attn_lat_kernel(q_ref, kv_ref, kpe_ref, kvc_ref, kpec_ref, o_ref):
    kpe = kpe_ref[...]
    kpec = kpec_ref[0, 0].astype(BF16)
    kpec = jnp.concatenate([kpec, jnp.zeros_like(kpec)], axis=1)
    for h in range(N_HEADS):
        lo = h * HEAD_PAD
        qh = q_ref[:, lo:lo + HEAD_PAD]
        k_own = jnp.concatenate([kv_ref[:, lo:lo + NOPE_DIM], kpe], axis=1)
        k_c = jnp.concatenate([kvc_ref[:, lo:lo + NOPE_DIM], kpec], axis=1)
        (p_c, p_o), inv = _softmax_parts([_dot_nt(qh, k_c), _dot_nt(qh, k_own)])
        o = _dot(p_c, kvc_ref[:, lo + NOPE_DIM:lo + HEAD_PAD]) + _dot(p_o, kv_ref[:, lo + NOPE_DIM:lo + HEAD_PAD])
        o_ref[:, h * V_DIM:(h + 1) * V_DIM] = (o * inv).astype(o_ref.dtype)


def _attn_lat(q, kv, kpe, kv_cache, cache_krope, layer, dims):
    tch, m_ctx = dims["tch"], dims["m_ctx"]
    DB, _, P, rd = cache_krope.shape
    bq = min(256, tch)
    nq = tch // bq
    dv = N_HEADS * V_DIM
    q_blk = lambda b, t: (m_ctx // bq + b * nq + t, 0)
    own_blk = lambda b, t: (m_ctx // tch + b, 0)
    return pl.pallas_call(
        _attn_lat_kernel,
        grid=(DB, nq),
        in_specs=[pl.BlockSpec((bq, q.shape[1]), q_blk),
                  pl.BlockSpec((tch, kv.shape[1]), own_blk),
                  pl.BlockSpec((tch, LANE), own_blk),
                  pl.BlockSpec((P, kv_cache.shape[1]), lambda b, t: (b, 0)),
                  pl.BlockSpec((1, 1, P, rd), lambda b, t: (b, layer, 0, 0))],
        out_specs=pl.BlockSpec((bq, dv), lambda b, t: (b * nq + t, 0)),
        out_shape=jax.ShapeDtypeStruct((DB * tch, dv), BF16),
        compiler_params=_params(2, 48),
        name="attn_lat",
    )(q, kv, kpe, kv_cache, cache_krope)


def kernel(x_prompt, x_sample, cache_ckv, cache_krope, c, c_ctx, w_ada, b_ada, g_norm1, g_norm2, w_in, w_fourier,
           pool_scale, w_pool, g_q, w_q_b, g_kv, w_kv_b, w_o_mla, w_out, w_gate, w_up, w_down, g_final):
    B, S, D = x_prompt.shape
    DB, DS, _ = x_sample.shape
    L = w_ada.shape[0]
    DF = w_fourier.shape[1]
    DP = pool_scale.shape[1]
    QL = g_q.shape[1]
    KVL = g_kv.shape[1]
    m_ctx, m_lat = B * S, DB * DS
    M = m_ctx + m_lat
    assert DS % S == 0 and m_ctx % DS == 0 and S & (S - 1) == 0 and DS & (DS - 1) == 0
    assert 1 + DB <= MOD_ROWS and DS % GRID_W == 0
    dims = dict(m_ctx=m_ctx, tch=DS, s_ctx=S)

    off_p, off_q, off_kv, off_pe = DF, DF + DP, DF + DP + QL, DF + DP + QL + KVL
    assert off_p % DP == 0 and off_q % QL == 0 and off_kv % KVL == 0 and off_pe % IN_BN == 0

    xs = (x_prompt.reshape(m_ctx, D), x_sample.reshape(m_lat, D))
    cond = jnp.zeros((MOD_ROWS, D), F32).at[0].set(c_ctx).at[1:1 + DB].set(c)
    b_ada3 = b_ada.reshape(L, 1, 6 * D)
    mod4 = _ada_mod(cond, w_ada, b_ada3, 0).reshape(1, MOD_ROWS, 1, 6 * D)
    cos, sin = _rope_tables(DS)
    w_in_t = jnp.swapaxes(w_in, 1, 2)
    wf_b, wp_b, wo_b = w_fourier.astype(BF16), w_pool.reshape(L, DP, D).astype(BF16), w_o_mla.astype(BF16)
    w_out_b = w_out.astype(BF16)

    ckv_layers, kpe_layers = [], []
    for l in range(L):
        w_kv = w_kv_b[l].astype(BF16)

        h = _norm_mod(xs, g_norm1[l:l + 1], mod4, 0, dims)
        z, kpe_raw, gates = _in_proj(h, w_in_t, l, off_pe, off_pe + ROPE_DIM, 2048)

        mixed = _dft(z, dims, DF)
        pooled = _pool(z, pool_scale[l], dims, DP, off_p // DP)
        q = _q_proj(z, g_q[l], w_q_b, l, cos, sin, dims, QL, off_q // QL)
        c_kv, kv, kpe = _kv_proj(z, kpe_raw, g_kv[l], w_kv, cos, sin, dims, KVL, off_kv // KVL)
        kv_cache = _kv_cache_proj(cache_ckv, w_kv, l)
        o_ctx = _attn_ctx(q, kv, kpe, dims)
        o_lat = _attn_lat(q, kv, kpe, kv_cache, cache_krope, l, dims)

        merged = _merge(mixed, pooled, (o_ctx, o_lat), wf_b, wp_b, wo_b, gates, l, 1024, 1024)
        x = _mm_resid(merged, w_out_b, xs, mod4, l, 2, 1024, 1024, dims, 60, "out_proj")

        h = _norm_mod((x,), g_norm2[l:l + 1], mod4, 3, dims)
        act, w_down_b = _swiglu_up(h, w_gate, w_up, w_down, l, 2048, 256)
        if l + 1 < L:
            x, mod_next = _mm_resid(act, w_down_b[None], (x,), mod4, 0, 5, 512, 512, dims, 60, "down_proj",
                                    ada_side=(cond, w_ada, b_ada3, l + 1))
            mod4 = mod_next.reshape(1, MOD_ROWS, 1, 6 * D)
        else:
            x = _mm_resid(act, w_down_b[None], (x,), mod4, 0, 5, 512, 512, dims, 56, "down_proj")
        xs = (x,)

        ckv_layers.append(c_kv[:m_ctx].reshape(B, S, KVL))
        kpe_layers.append(kpe_raw[:m_ctx, :ROPE_DIM].reshape(B, S, ROPE_DIM))

    y_prompt = _final_norm(x, g_final, 0, m_ctx).reshape(B, S, D)
    y_sample = _final_norm(x, g_final, m_ctx, m_lat).reshape(DB, DS, D)
    return (y_prompt, y_sample, jnp.stack(ckv_layers, axis=1), jnp.stack(kpe_layers, axis=1))
```

```python
import functools

import numpy as np
import jax
import jax.numpy as jnp
from jax.experimental import pallas as pl
from jax.experimental.pallas import tpu as pltpu

N_HEADS = 16
NOPE_DIM = 128
ROPE_DIM = 64
V_DIM = 128
HEAD_PAD = 256
GRID_W = 64
EPS = 1e-6
ROPE_BASE = 10000.0
SM_SCALE = (NOPE_DIM + ROPE_DIM) ** -0.5
Q_SCALE = SM_SCALE * float(np.log2(np.e))
POOL_WINDOWS = (2, 4, 8, 16)
N_FOURIER_GROUPS = 4
MOD_ROWS = 8
LANE = 128

BF16 = jnp.bfloat16
F32 = jnp.float32


def _params(n_grid, vmem_mb):
    return pltpu.CompilerParams(dimension_semantics=("arbitrary",) * n_grid,
                                vmem_limit_bytes=vmem_mb * 2 ** 20)


def _dot(a, b):
    return jnp.dot(a, b, preferred_element_type=F32)


def _dot_nt(a, b):
    return jax.lax.dot_general(a, b, (((1,), (1,)), ((), ())), preferred_element_type=F32)


SLAB_ROWS = 256


def _slabs(n_rows, slab_rows=SLAB_ROWS):
    r = int(np.gcd(slab_rows, n_rows))
    return [slice(s * r, (s + 1) * r) for s in range(n_rows // r)]


def _mod_row(i, bm, m_ctx, tch):
    start = i * bm
    return jnp.where(start < m_ctx, 0, 1 + (start - m_ctx) // tch)


def _ada_kernel(c_ref, w_ref, b_ref, o_ref):
    c = c_ref[...]
    a = (c * jax.nn.sigmoid(c)).astype(BF16)
    o_ref[0] = _dot(a, w_ref[0].astype(BF16)) + b_ref[0]


def _ada_mod(cond, w_ada, b_ada3, layer):
    _, D, N = w_ada.shape
    bn = 512
    return pl.pallas_call(
        _ada_kernel,
        grid=(N // bn,),
        in_specs=[pl.BlockSpec((MOD_ROWS, D), lambda j: (0, 0)),
                  pl.BlockSpec((1, D, bn), lambda j: (layer, 0, j)),
                  pl.BlockSpec((1, 1, bn), lambda j: (layer, 0, j))],
        out_specs=pl.BlockSpec((1, MOD_ROWS, bn), lambda j: (0, 0, j)),
        out_shape=jax.ShapeDtypeStruct((1, MOD_ROWS, N), F32),
        compiler_params=_params(1, 40),
        name="ada_mod",
    )(cond, w_ada, b_ada3)


def _ada_side_specs(cond, w_ada, b_ada3, layer, n_steps, step_of):
    _, D, N = w_ada.shape
    bn = min(t for t in range(2 * LANE, N + 1, 2 * LANE) if N % t == 0 and N // t <= n_steps)
    n_side = N // bn
    blk = lambda *g: jnp.minimum(step_of(*g), n_side - 1)
    in_specs = [pl.BlockSpec((MOD_ROWS, D), lambda *g: (0, 0)),
                pl.BlockSpec((1, D, bn), lambda *g: (layer, 0, blk(*g))),
                pl.BlockSpec((1, 1, bn), lambda *g: (layer, 0, blk(*g)))]
    out_spec = pl.BlockSpec((1, MOD_ROWS, bn), lambda *g: (0, 0, blk(*g)))
    out_shape = jax.ShapeDtypeStruct((1, MOD_ROWS, N), F32)
    return n_side, in_specs, out_spec, out_shape


def _row_specs(xs, bm, bn, n_grid):
    idx = (lambda *g: (g[-1], g[0])) if n_grid == 2 else (lambda *g: (g[-1], 0))
    if len(xs) == 1:
        return [pl.BlockSpec((bm, bn), idx)]
    n_a = xs[0].shape[0] // bm
    last_b = xs[1].shape[0] // bm - 1
    spec_a = pl.BlockSpec((bm, bn), lambda *g: (jnp.minimum(idx(*g)[0], n_a - 1), idx(*g)[1]))
    spec_b = pl.BlockSpec((bm, bn), lambda *g: (jnp.clip(idx(*g)[0] - n_a, 0, last_b), idx(*g)[1]))
    return [spec_a, spec_b]


def _norm_mod_kernel(*refs, n_x, n_a):
    x_refs, (g_ref, sh_ref, sc_ref, o_ref) = refs[:n_x], refs[n_x:]
    x = x_refs[0][...] if n_x == 1 else jnp.where(pl.program_id(0) < n_a, x_refs[0][...], x_refs[1][...])
    r = jax.lax.rsqrt(jnp.mean(x * x, axis=-1, keepdims=True) + EPS)
    y = x * r * g_ref[...]
    o_ref[...] = (y * (1.0 + sc_ref[0, 0]) + sh_ref[0, 0]).astype(o_ref.dtype)


def _norm_mod(xs, g, mod4, shift_blk, dims):
    M = sum(x.shape[0] for x in xs)
    D = xs[0].shape[1]
    bm = min(512, dims["tch"])
    row = functools.partial(_mod_row, bm=bm, m_ctx=dims["m_ctx"], tch=dims["tch"])
    return pl.pallas_call(
        functools.partial(_norm_mod_kernel, n_x=len(xs), n_a=xs[0].shape[0] // bm),
        grid=(M // bm,),
        in_specs=_row_specs(xs, bm, D, 1) + [
            pl.BlockSpec((1, D), lambda i: (0, 0)),
            pl.BlockSpec((1, 1, 1, D), lambda i: (0, row(i), 0, shift_blk)),
            pl.BlockSpec((1, 1, 1, D), lambda i: (0, row(i), 0, shift_blk + 1))],
        out_specs=pl.BlockSpec((bm, D), lambda i: (i, 0)),
        out_shape=jax.ShapeDtypeStruct((M, D), BF16),
        compiler_params=_params(1, 56),
        name="norm_mod",
    )(*xs, g, mod4, mod4)


def _final_norm_kernel(x_ref, g_ref, o_ref):
    x = x_ref[...]
    r = jax.lax.rsqrt(jnp.mean(x * x, axis=-1, keepdims=True) + EPS)
    o_ref[...] = x * r * g_ref[...]


def _final_norm(x, g, row0, rows):
    D = x.shape[1]
    bm = int(np.gcd(512, np.gcd(rows, row0 if row0 else rows)))
    blk0 = row0 // bm
    return pl.pallas_call(
        _final_norm_kernel,
        grid=(rows // bm,),
        in_specs=[pl.BlockSpec((bm, D), lambda i: (blk0 + i, 0)),
                  pl.BlockSpec((1, D), lambda i: (0, 0))],
        out_specs=pl.BlockSpec((bm, D), lambda i: (i, 0)),
        out_shape=jax.ShapeDtypeStruct((rows, D), F32),
        compiler_params=_params(1, 48),
        name="final_norm",
    )(x, g.reshape(1, D))


IN_BN = 512


def _in_proj_kernel(h_ref, w_ref, wk_ref, z_ref, kpe_ref, g_ref, *, n_small_blk):
    j = pl.program_id(1)

    @pl.when(j < n_small_blk)
    def _():
        w = w_ref[0].astype(BF16)
        for rows in _slabs(h_ref.shape[0]):
            z_ref[rows, :] = _dot_nt(h_ref[rows, :], w)

    @pl.when(j == n_small_blk)
    def _():
        kpe_ref[...] = _dot_nt(h_ref[...], wk_ref[0].astype(BF16))

    @pl.when(j >= n_small_blk)
    def _():
        w = w_ref[0].astype(BF16)
        for rows in _slabs(h_ref.shape[0]):
            g_ref[rows, :] = jax.nn.sigmoid(_dot_nt(h_ref[rows, :], w)).astype(g_ref.dtype)


def _in_proj(h, w_t, layer, off_pe, n_small, bm):
    M, K = h.shape
    N = w_t.shape[1]
    bm = min(bm, M)
    n_small_blk = off_pe // IN_BN
    n_gate_blk = (N - n_small) // IN_BN
    assert off_pe % IN_BN == 0 and (N - n_small) % IN_BN == 0 and n_small % 8 == 0 and off_pe % LANE == 0

    def w_row(j):
        row = jnp.where(j < n_small_blk, j * IN_BN, n_small + (j - n_small_blk) * IN_BN)
        return pl.multiple_of(row, 8)

    return pl.pallas_call(
        functools.partial(_in_proj_kernel, n_small_blk=n_small_blk),
        grid=(M // bm, n_small_blk + n_gate_blk),
        in_specs=[pl.BlockSpec((bm, K), lambda i, j: (i, 0), pipeline_mode=pl.Buffered(1)),
                  pl.BlockSpec((pl.Element(1), pl.Element(IN_BN), pl.Element(K)), lambda i, j: (layer, w_row(j), 0)),
                  pl.BlockSpec((1, LANE, K), lambda i, j: (layer, off_pe // LANE, 0), pipeline_mode=pl.Buffered(1))],
        out_specs=[pl.BlockSpec((bm, IN_BN), lambda i, j: (i, jnp.minimum(j, n_small_blk - 1))),
                   pl.BlockSpec((bm, LANE), lambda i, j: (i, 0)),
                   pl.BlockSpec((bm, IN_BN), lambda i, j: (i, jnp.maximum(j - n_small_blk, 0)))],
        out_shape=[jax.ShapeDtypeStruct((M, off_pe), F32),
                   jax.ShapeDtypeStruct((M, LANE), F32),
                   jax.ShapeDtypeStruct((M, n_gate_blk * IN_BN), BF16)],
        compiler_params=_params(2, 60),
        name="in_proj",
    )(h, w_t, w_t)


def _mm_resid_kernel(a_ref, w_ref, *refs, n_x, n_a, n_side):
    x_refs, rest = refs[:n_x], refs[n_x:]
    gate_ref, o_ref = rest[0], rest[-2 if n_side else -1]
    first = pl.program_id(1) < n_a
    for rows in _slabs(a_ref.shape[0]):
        x = x_refs[0][rows, :] if n_x == 1 else jnp.where(first, x_refs[0][rows, :], x_refs[1][rows, :])
        o_ref[rows, :] = x + gate_ref[0, 0] * _dot(a_ref[rows, :], w_ref[0])

    if n_side:
        c_ref, wa_ref, ba_ref, mod_o = rest[1], rest[2], rest[3], rest[-1]

        @pl.when(pl.program_id(0) * pl.num_programs(1) + pl.program_id(1) < n_side)
        def _():
            _ada_kernel(c_ref, wa_ref, ba_ref, mod_o)


def _mm_resid(a, w, xs, mod4, layer, gate_blk, bm, bn, dims, vmem_mb, name, ada_side=None):
    M, K = a.shape
    N = w.shape[2]
    bm = min(bm, dims["tch"])
    row = functools.partial(_mod_row, bm=bm, m_ctx=dims["m_ctx"], tch=dims["tch"])
    nb, ni = N // bn, M // bm
    in_specs = [pl.BlockSpec((bm, K), lambda j, i: (i, 0)),
                pl.BlockSpec((1, K, bn), lambda j, i: (layer, 0, j))] + _row_specs(xs, bm, bn, 2) + [
                pl.BlockSpec((1, 1, 1, bn), lambda j, i: (0, row(i), 0, gate_blk * nb + j))]
    out_specs = [pl.BlockSpec((bm, bn), lambda j, i: (i, j))]
    out_shape = [jax.ShapeDtypeStruct((M, N), F32)]
    args = [a, w, *xs, mod4]
    n_side = 0
    if ada_side is not None:
        n_side, side_in, side_out, side_shape = _ada_side_specs(*ada_side, nb * ni, lambda j, i: j * ni + i)
        in_specs += side_in
        out_specs.append(side_out)
        out_shape.append(side_shape)
        args += list(ada_side[:3])
    res = pl.pallas_call(
        functools.partial(_mm_resid_kernel, n_x=len(xs), n_a=xs[0].shape[0] // bm, n_side=n_side),
        grid=(nb, ni),
        in_specs=in_specs,
        out_specs=out_specs,
        out_shape=out_shape,
        compiler_params=_params(2, vmem_mb),
        name=name,
    )(*args)
    return res if n_side else res[0]


def _swiglu_kernel(a_ref, wg_ref, wu_ref, wd_ref, o_ref, wd_o, *, n_cast):
    wg = wg_ref[0].astype(BF16)
    wu = wu_ref[0].astype(BF16)
    for rows in _slabs(a_ref.shape[0]):
        a = a_ref[rows, :]
        g = _dot(a, wg)
        u = _dot(a, wu)
        o_ref[rows, :] = (g * jax.nn.sigmoid(g) * u).astype(o_ref.dtype)

    @pl.when(pl.program_id(0) * pl.num_programs(1) + pl.program_id(1) < n_cast)
    def _():
        wd_o[...] = wd_ref[0].astype(wd_o.dtype)


def _swiglu_up(a, wg, wu, wd, layer, bm, bn):
    M, K = a.shape
    N = wg.shape[2]
    KD, ND = wd.shape[1:]
    bm = min(bm, M)
    nj = N // bn
    n_steps = (M // bm) * nj
    n_cast = max(n for n in range(1, n_steps + 1) if KD % n == 0 and (KD // n) % 16 == 0)
    rb = KD // n_cast
    cast_blk = lambda i, j: jnp.minimum(i * nj + j, n_cast - 1)
    return pl.pallas_call(
        functools.partial(_swiglu_kernel, n_cast=n_cast),
        grid=(M // bm, nj),
        in_specs=[pl.BlockSpec((bm, K), lambda i, j: (i, 0)),
                  pl.BlockSpec((1, K, bn), lambda i, j: (layer, 0, j)),
                  pl.BlockSpec((1, K, bn), lambda i, j: (layer, 0, j)),
                  pl.BlockSpec((1, rb, ND), lambda i, j: (layer, cast_blk(i, j), 0))],
        out_specs=[pl.BlockSpec((bm, bn), lambda i, j: (i, j)),
                   pl.BlockSpec((rb, ND), lambda i, j: (cast_blk(i, j), 0))],
        out_shape=[jax.ShapeDtypeStruct((M, N), BF16),
                   jax.ShapeDtypeStruct((KD, ND), BF16)],
        compiler_params=_params(2, 60),
        name="swiglu_up",
    )(a, wg, wu, wd)


def _merge_kernel(af_ref, ap_ref, aoc_ref, aol_ref, wf_ref, wp_ref, wo_ref, g0_ref, g1_ref, g2_ref, o_ref, *, n_a):
    first = pl.program_id(1) < n_a
    for rows in _slabs(af_ref.shape[0]):
        ao = jnp.where(first, aoc_ref[rows, :], aol_ref[rows, :])
        y = g0_ref[rows, :].astype(F32) * _dot(af_ref[rows, :], wf_ref[0])
        y = y + g1_ref[rows, :].astype(F32) * _dot(ap_ref[rows, :], wp_ref[0])
        y = y + g2_ref[rows, :].astype(F32) * _dot(ao, wo_ref[0])
        o_ref[rows, :] = y.astype(o_ref.dtype)


def _merge(af, ap, aos, wf, wp, wo, gates, layer, bm, bn):
    M = af.shape[0]
    N = wf.shape[2]
    bm = min(bm, aos[0].shape[0], aos[1].shape[0])
    nb = N // bn
    a_spec = lambda a: pl.BlockSpec((bm, a.shape[1]), lambda j, i: (i, 0))
    w_spec = lambda w: pl.BlockSpec((1, w.shape[1], bn), lambda j, i: (layer, 0, j))
    g_spec = lambda k: pl.BlockSpec((bm, bn), lambda j, i: (i, k * nb + j))
    return pl.pallas_call(
        functools.partial(_merge_kernel, n_a=aos[0].shape[0] // bm),
        grid=(nb, M // bm),
        in_specs=[a_spec(af), a_spec(ap)] + _row_specs(aos, bm, aos[0].shape[1], 1) + [
                  w_spec(wf), w_spec(wp), w_spec(wo), g_spec(0), g_spec(1), g_spec(2)],
        out_specs=pl.BlockSpec((bm, bn), lambda j, i: (i, j)),
        out_shape=jax.ShapeDtypeStruct((M, N), BF16),
        compiler_params=_params(2, 60),
        name="merge",
    )(af, ap, *aos, wf, wp, wo, gates, gates, gates)


def _dft_group(f_ref, ch_ref, pos_ref, o_ref, g, s, fg):
    tch = f_ref.shape[0]
    scale = 1.0 / float(np.sqrt(s * fg))
    x = f_ref[:, g * fg:(g + 1) * fg].astype(BF16)
    xcs = _dot(x, ch_ref[...]).astype(BF16)
    for q in range(tch // s):
        rows = slice(q * s, (q + 1) * s)
        stacked = jnp.concatenate([xcs[rows, :fg], xcs[rows, fg:]], axis=0)
        y = _dot(pos_ref[...], stacked) * scale
        o_ref[rows, g * fg:(g + 1) * fg] = y.astype(o_ref.dtype)


def _pool_group(p_ref, ps_ref, o_ref, gi, s, pg):
    tch = p_ref.shape[0]
    w = POOL_WINDOWS[gi]
    half = w // 2
    assert w == 2 * half and half & (half - 1) == 0
    pos = jax.lax.broadcasted_iota(jnp.int32, (tch, pg), 0) & (s - 1)
    cols = slice(gi * pg, (gi + 1) * pg)
    x = p_ref[:, cols]
    before = jnp.where(pos >= 1, pltpu.roll(x, 1, 0), 0.0)
    after = x
    m = 1
    while m < half:
        before = before + jnp.where(pos >= m, pltpu.roll(before, m, 0), 0.0)
        after = after + jnp.where(pos + m < s, pltpu.roll(after, tch - m, 0), 0.0)
        m *= 2
    cnt = (jnp.minimum(pos + half, s) - jnp.maximum(pos - half, 0)).astype(F32)
    o_ref[:, cols] = (((before + after) / cnt - x) * ps_ref[:, cols]).astype(o_ref.dtype)


def _dft_kernel(f_ref, ch_ref, pc_ref, pl_ref, o_ref, *, n_ctx_chunks, s_ctx, fg):
    i = pl.program_id(0)

    @pl.when(i < n_ctx_chunks)
    def _():
        for g in range(N_FOURIER_GROUPS):
            _dft_group(f_ref, ch_ref, pc_ref, o_ref, g, s_ctx, fg)

    @pl.when(i >= n_ctx_chunks)
    def _():
        for g in range(N_FOURIER_GROUPS):
            _dft_group(f_ref, ch_ref, pl_ref, o_ref, g, f_ref.shape[0], fg)


def _pool_kernel(p_ref, ps_ref, o_ref, *, n_ctx_chunks, s_ctx, pg):
    i = pl.program_id(0)

    @pl.when(i < n_ctx_chunks)
    def _():
        for gi in range(len(POOL_WINDOWS)):
            _pool_group(p_ref, ps_ref, o_ref, gi, s_ctx, pg)

    @pl.when(i >= n_ctx_chunks)
    def _():
        for gi in range(len(POOL_WINDOWS)):
            _pool_group(p_ref, ps_ref, o_ref, gi, p_ref.shape[0], pg)


def _dft_tables(s_ctx, s_lat, fg):
    def cs(n):
        k = np.arange(n, dtype=np.int64)
        ang = 2.0 * np.pi * ((k[:, None] * k[None, :]) % n).astype(np.float64) / n
        return np.cos(ang), np.sin(ang)
    cc, sc = cs(fg)
    chan = np.concatenate([cc, sc], axis=1)
    pos = []
    for s in (s_ctx, s_lat):
        c, sn = cs(s)
        pos.append(np.concatenate([c, -sn], axis=1))
    return (jnp.asarray(chan, BF16), jnp.asarray(pos[0], BF16), jnp.asarray(pos[1], BF16))


def _dft(z, dims, df):
    M = z.shape[0]
    tch, s_ctx = dims["tch"], dims["s_ctx"]
    fg = df // N_FOURIER_GROUPS
    chan, pos_c, pos_l = _dft_tables(s_ctx, tch, fg)
    full = lambda a: pl.BlockSpec(a.shape, lambda i: (0, 0))
    return pl.pallas_call(
        functools.partial(_dft_kernel, n_ctx_chunks=dims["m_ctx"] // tch, s_ctx=s_ctx, fg=fg),
        grid=(M // tch,),
        in_specs=[pl.BlockSpec((tch, df), lambda i: (i, 0)), full(chan), full(pos_c), full(pos_l)],
        out_specs=pl.BlockSpec((tch, df), lambda i: (i, 0)),
        out_shape=jax.ShapeDtypeStruct((M, df), BF16),
        compiler_params=_params(1, 48),
        name="dft_mix",
    )(z, chan, pos_c, pos_l)


def _pool(z, pool_scale_l, dims, dp, col_blk):
    M = z.shape[0]
    tch = dims["tch"]
    return pl.pallas_call(
        functools.partial(_pool_kernel, n_ctx_chunks=dims["m_ctx"] // tch, s_ctx=dims["s_ctx"],
                          pg=dp // len(POOL_WINDOWS)),
        grid=(M // tch,),
        in_specs=[pl.BlockSpec((tch, dp), lambda i: (i, col_blk)),
                  pl.BlockSpec((1, dp), lambda i: (0, 0))],
        out_specs=pl.BlockSpec((tch, dp), lambda i: (i, 0)),
        out_shape=jax.ShapeDtypeStruct((M, dp), BF16),
        compiler_params=_params(1, 32),
        name="pool",
    )(z, pool_scale_l.reshape(1, dp))


def _rope_tables(n_tokens):
    rows = n_tokens // GRID_W
    row = np.repeat(np.arange(rows, dtype=np.float64), GRID_W)
    col = np.tile(np.arange(GRID_W, dtype=np.float64), rows)
    inv = ROPE_BASE ** (-np.arange(ROPE_DIM // 4, dtype=np.float64) * (4.0 / ROPE_DIM))
    ar = row[:, None] * inv[None, :]
    ac = col[:, None] * inv[None, :]
    ang = np.concatenate([ar, ar, ac, ac], axis=1)
    sign = np.tile(np.concatenate([-np.ones(ROPE_DIM // 4), np.ones(ROPE_DIM // 4)]), 2)
    cos = np.concatenate([np.cos(ang), np.ones_like(ang)], axis=1)
    sin = np.concatenate([np.sin(ang) * sign[None, :], np.zeros_like(ang)], axis=1)
    return jnp.asarray(cos, F32), jnp.asarray(sin, F32)


def _rope(x, cos, sin):
    q = ROPE_DIM // 4
    lane = jax.lax.broadcasted_iota(jnp.int32, x.shape, 1)
    first = (lane & (2 * q - 1)) < q
    swapped = jnp.where(first, pltpu.roll(x, LANE - q, 1), pltpu.roll(x, q, 1))
    return x * cos + swapped * sin


def _q_kernel(cq_ref, g_ref, w_ref, cos_ref, sin_ref, o_ref, wp_ref, *, n_ctx_tiles):
    i = pl.program_id(0)

    @pl.when(i == 0)
    def _():
        hd = NOPE_DIM + ROPE_DIM
        for h in range(N_HEADS):
            wp_ref[:, h * HEAD_PAD:h * HEAD_PAD + hd] = w_ref[0, :, h * hd:(h + 1) * hd].astype(BF16)
            wp_ref[:, h * HEAD_PAD + hd:(h + 1) * HEAD_PAD] = jnp.zeros((wp_ref.shape[0], HEAD_PAD - hd), BF16)

    def q_rows(rows):
        x = cq_ref[rows, :]
        r = jax.lax.rsqrt(jnp.mean(x * x, axis=-1, keepdims=True) + EPS)
        return _dot((x * r * g_ref[...]).astype(BF16), wp_ref[...]) * Q_SCALE

    @pl.when(i < n_ctx_tiles)
    def _():
        for rows in _slabs(cq_ref.shape[0]):
            o_ref[rows, :] = q_rows(rows).astype(o_ref.dtype)

    @pl.when(i >= n_ctx_tiles)
    def _():
        for rows in _slabs(cq_ref.shape[0]):
            q = q_rows(rows)
            cos, sin = cos_ref[rows, :], sin_ref[rows, :]
            for h in range(N_HEADS):
                lo = h * HEAD_PAD
                o_ref[rows, lo:lo + NOPE_DIM] = q[:, lo:lo + NOPE_DIM].astype(o_ref.dtype)
                pe = q[:, lo + NOPE_DIM:lo + HEAD_PAD]
                o_ref[rows, lo + NOPE_DIM:lo + HEAD_PAD] = _rope(pe, cos, sin).astype(o_ref.dtype)


def _q_proj(z, g_q_l, w_q_b, layer, cos, sin, dims, ql, col_blk):
    M = z.shape[0]
    N = N_HEADS * HEAD_PAD
    bm = min(512, dims["tch"])
    n_ctx_tiles = dims["m_ctx"] // bm
    per_seq = dims["tch"] // bm
    pos_blk = lambda i: (jnp.maximum(i - n_ctx_tiles, 0) % per_seq, 0)
    return pl.pallas_call(
        functools.partial(_q_kernel, n_ctx_tiles=n_ctx_tiles),
        grid=(M // bm,),
        in_specs=[pl.BlockSpec((bm, ql), lambda i: (i, col_blk)),
                  pl.BlockSpec((1, ql), lambda i: (0, 0)),
                  pl.BlockSpec((1,) + w_q_b.shape[1:], lambda i: (layer, 0, 0), pipeline_mode=pl.Buffered(1)),
                  pl.BlockSpec((bm, LANE), pos_blk),
                  pl.BlockSpec((bm, LANE), pos_blk)],
        out_specs=pl.BlockSpec((bm, N), lambda i: (i, 0)),
        out_shape=jax.ShapeDtypeStruct((M, N), BF16),
        scratch_shapes=[pltpu.VMEM((ql, N), BF16)],
        compiler_params=_params(1, 56),
        name="q_proj",
    )(z, g_q_l.reshape(1, ql), w_q_b, cos, sin)


def _kv_kernel(ckv_ref, g_ref, w_ref, kpe_ref, cos_ref, sin_ref, ckv_o, kv_o, kpe_o, *, n_ctx_tiles):
    i = pl.program_id(0)
    x = ckv_ref[...]
    r = jax.lax.rsqrt(jnp.mean(x * x, axis=-1, keepdims=True) + EPS)
    c = x * r * g_ref[...]
    ckv_o[...] = c
    kv_o[...] = _dot(c.astype(BF16), w_ref[...]).astype(kv_o.dtype)
    kpe = kpe_ref[...]
    kpe = jnp.where(jax.lax.broadcasted_iota(jnp.int32, kpe.shape, 1) < ROPE_DIM, kpe, 0.0)

    @pl.when(i < n_ctx_tiles)
    def _():
        kpe_o[...] = kpe.astype(kpe_o.dtype)

    @pl.when(i >= n_ctx_tiles)
    def _():
        kpe_o[...] = _rope(kpe, cos_ref[...], sin_ref[...]).astype(kpe_o.dtype)


def _kv_proj(z, kpe_raw, g_kv_l, w_kv, cos, sin, dims, kvl, ckv_blk):
    M = z.shape[0]
    N = w_kv.shape[1]
    bm = min(512, dims["tch"])
    n_ctx_tiles = dims["m_ctx"] // bm
    per_seq = dims["tch"] // bm
    pos_blk = lambda i: (jnp.maximum(i - n_ctx_tiles, 0) % per_seq, 0)
    return pl.pallas_call(
        functools.partial(_kv_kernel, n_ctx_tiles=n_ctx_tiles),
        grid=(M // bm,),
        in_specs=[pl.BlockSpec((bm, kvl), lambda i: (i, ckv_blk)),
                  pl.BlockSpec((1, kvl), lambda i: (0, 0)),
                  pl.BlockSpec((kvl, N), lambda i: (0, 0)),
                  pl.BlockSpec((bm, LANE), lambda i: (i, 0)),
                  pl.BlockSpec((bm, LANE), pos_blk),
                  pl.BlockSpec((bm, LANE), pos_blk)],
        out_specs=[pl.BlockSpec((bm, kvl), lambda i: (i, 0)),
                   pl.BlockSpec((bm, N), lambda i: (i, 0)),
                   pl.BlockSpec((bm, LANE), lambda i: (i, 0))],
        out_shape=[jax.ShapeDtypeStruct((M, kvl), F32),
                   jax.ShapeDtypeStruct((M, N), BF16),
                   jax.ShapeDtypeStruct((M, LANE), BF16)],
        compiler_params=_params(1, 48),
        name="kv_proj",
    )(z, g_kv_l.reshape(1, kvl), w_kv, kpe_raw, cos, sin)


def _kv_cache_kernel(c_ref, w_ref, o_ref):
    o_ref[...] = _dot(c_ref[0, 0].astype(BF16), w_ref[...]).astype(o_ref.dtype)


def _kv_cache_proj(cache_ckv, w_kv, layer):
    DB, _, P, kvl = cache_ckv.shape
    N = w_kv.shape[1]
    return pl.pallas_call(
        _kv_cache_kernel,
        grid=(DB,),
        in_specs=[pl.BlockSpec((1, 1, P, kvl), lambda b: (b, layer, 0, 0)),
                  pl.BlockSpec((kvl, N), lambda b: (0, 0))],
        out_specs=pl.BlockSpec((P, N), lambda b: (b, 0)),
        out_shape=jax.ShapeDtypeStruct((DB * P, N), BF16),
        compiler_params=_params(1, 32),
        name="kv_cache_proj",
    )(cache_ckv, w_kv)


def _softmax_parts(parts):
    m = functools.reduce(jnp.maximum, [jnp.max(s, axis=-1, keepdims=True) for s in parts])
    es = [jnp.exp2(s - m) for s in parts]
    tot = functools.reduce(jnp.add, [jnp.sum(e, axis=-1, keepdims=True) for e in es])
    return [e.astype(BF16) for e in es], 1.0 / tot


def _attn_ctx_kernel(q_ref, kv_ref, kpe_ref, o_ref):
    kpe = kpe_ref[...]
    for h in range(N_HEADS):
        lo = h * HEAD_PAD
        k = jnp.concatenate([kv_ref[:, lo:lo + NOPE_DIM], kpe], axis=1)
        s = _dot_nt(q_ref[:, lo:lo + HEAD_PAD], k)
        (p,), inv = _softmax_parts([s])
        o = _dot(p, kv_ref[:, lo + NOPE_DIM:lo + HEAD_PAD]) * inv
        o_ref[:, h * V_DIM:(h + 1) * V_DIM] = o.astype(o_ref.dtype)


def _attn_ctx(q, kv, kpe, dims):
    M = q.shape[0]
    s = dims["s_ctx"]
    n = dims["m_ctx"] // s
    dv = N_HEADS * V_DIM
    return pl.pallas_call(
        _attn_ctx_kernel,
        grid=(n,),
        in_specs=[pl.BlockSpec((s, q.shape[1]), lambda b: (b, 0)),
                  pl.BlockSpec((s, kv.shape[1]), lambda b: (b, 0)),
                  pl.BlockSpec((s, LANE), lambda b: (b, 0))],
        out_specs=pl.BlockSpec((s, dv), lambda b: (b, 0)),
        out_shape=jax.ShapeDtypeStruct((dims["m_ctx"], dv), BF16),
        compiler_params=_params(1, 32),
        name="attn_ctx",
    )(q, kv, kpe)


def _attn_lat_kernel(q_ref, kv_ref, kpe_ref, kvc_ref, kpec_ref, o_ref):
    kpe = kpe_ref[...]
    kpec = kpec_ref[0, 0].astype(BF16)
    kpec = jnp.concatenate([kpec, jnp.zeros_like(kpec)], axis=1)
    for h in range(N_HEADS):
        lo = h * HEAD_PAD
        qh = q_ref[:, lo:lo + HEAD_PAD]
        k_own = jnp.concatenate([kv_ref[:, lo:lo + NOPE_DIM], kpe], axis=1)
        k_c = jnp.concatenate([kvc_ref[:, lo:lo + NOPE_DIM], kpec], axis=1)
        (p_c, p_o), inv = _softmax_parts([_dot_nt(qh, k_c), _dot_nt(qh, k_own)])
        o = _dot(p_c, kvc_ref[:, lo + NOPE_DIM:lo + HEAD_PAD]) + _dot(p_o, kv_ref[:, lo + NOPE_DIM:lo + HEAD_PAD])
        o_ref[:, h * V_DIM:(h + 1) * V_DIM] = (o * inv).astype(o_ref.dtype)


def _attn_lat(q, kv, kpe, kv_cache, cache_krope, layer, dims):
    tch, m_ctx = dims["tch"], dims["m_ctx"]
    DB, _, P, rd = cache_krope.shape
    bq = min(256, tch)
    nq = tch // bq
    dv = N_HEADS * V_DIM
    q_blk = lambda b, t: (m_ctx // bq + b * nq + t, 0)
    own_blk = lambda b, t: (m_ctx // tch + b, 0)
    return pl.pallas_call(
        _attn_lat_kernel,
        grid=(DB, nq),
        in_specs=[pl.BlockSpec((bq, q.shape[1]), q_blk),
                  pl.BlockSpec((tch, kv.shape[1]), own_blk),
                  pl.BlockSpec((tch, LANE), own_blk),
                  pl.BlockSpec((P, kv_cache.shape[1]), lambda b, t: (b, 0)),
                  pl.BlockSpec((1, 1, P, rd), lambda b, t: (b, layer, 0, 0))],
        out_specs=pl.BlockSpec((bq, dv), lambda b, t: (b * nq + t, 0)),
        out_shape=jax.ShapeDtypeStruct((DB * tch, dv), BF16),
        compiler_params=_params(2, 48),
        name="attn_lat",
    )(q, kv, kpe, kv_cache, cache_krope)


def kernel(x_prompt, x_sample, cache_ckv, cache_krope, c, c_ctx, w_ada, b_ada, g_norm1, g_norm2, w_in, w_fourier,
           pool_scale, w_pool, g_q, w_q_b, g_kv, w_kv_b, w_o_mla, w_out, w_gate, w_up, w_down, g_final):
    B, S, D = x_prompt.shape
    DB, DS, _ = x_sample.shape
    L = w_ada.shape[0]
    DF = w_fourier.shape[1]
    DP = pool_scale.shape[1]
    QL = g_q.shape[1]
    KVL = g_kv.shape[1]
    m_ctx, m_lat = B * S, DB * DS
    M = m_ctx + m_lat
    assert DS % S == 0 and m_ctx % DS == 0 and S & (S - 1) == 0 and DS & (DS - 1) == 0
    assert 1 + DB <= MOD_ROWS and DS % GRID_W == 0
    dims = dict(m_ctx=m_ctx, tch=DS, s_ctx=S)

    off_p, off_q, off_kv, off_pe = DF, DF + DP, DF + DP + QL, DF + DP + QL + KVL
    assert off_p % DP == 0 and off_q % QL == 0 and off_kv % KVL == 0 and off_pe % IN_BN == 0

    xs = (x_prompt.reshape(m_ctx, D), x_sample.reshape(m_lat, D))
    cond = jnp.zeros((MOD_ROWS, D), F32).at[0].set(c_ctx).at[1:1 + DB].set(c)
    b_ada3 = b_ada.reshape(L, 1, 6 * D)
    mod4 = _ada_mod(cond, w_ada, b_ada3, 0).reshape(1, MOD_ROWS, 1, 6 * D)
    cos, sin = _rope_tables(DS)
    w_in_t = jnp.swapaxes(w_in, 1, 2)
    wf_b, wp_b, wo_b = w_fourier.astype(BF16), w_pool.reshape(L, DP, D).astype(BF16), w_o_mla.astype(BF16)
    w_out_b = w_out.astype(BF16)

    ckv_layers, kpe_layers = [], []
    for l in range(L):
        w_kv = w_kv_b[l].astype(BF16)

        h = _norm_mod(xs, g_norm1[l:l + 1], mod4, 0, dims)
        z, kpe_raw, gates = _in_proj(h, w_in_t, l, off_pe, off_pe + ROPE_DIM, 2048)

        mixed = _dft(z, dims, DF)
        pooled = _pool(z, pool_scale[l], dims, DP, off_p // DP)
        q = _q_proj(z, g_q[l], w_q_b, l, cos, sin, dims, QL, off_q // QL)
        c_kv, kv, kpe = _kv_proj(z, kpe_raw, g_kv[l], w_kv, cos, sin, dims, KVL, off_kv // KVL)
        kv_cache = _kv_cache_proj(cache_ckv, w_kv, l)
        o_ctx = _attn_ctx(q, kv, kpe, dims)
        o_lat = _attn_lat(q, kv, kpe, kv_cache, cache_krope, l, dims)

        merged = _merge(mixed, pooled, (o_ctx, o_lat), wf_b, wp_b, wo_b, gates, l, 1024, 1024)
        x = _mm_resid(merged, w_out_b, xs, mod4, l, 2, 1024, 1024, dims, 60, "out_proj")

        h = _norm_mod((x,), g_norm2[l:l + 1], mod4, 3, dims)
        act, w_down_b = _swiglu_up(h, w_gate, w_up, w_down, l, 2048, 256)
        if l + 1 < L:
            x, mod_next = _mm_resid(act, w_down_b[None], (x,), mod4, 0, 5, 512, 512, dims, 60, "down_proj",
                                    ada_side=(cond, w_ada, b_ada3, l + 1))
            mod4 = mod_next.reshape(1, MOD_ROWS, 1, 6 * D)
        else:
            x = _mm_resid(act, w_down_b[None], (x,), mod4, 0, 5, 512, 512, dims, 56, "down_proj")
        xs = (x,)

        ckv_layers.append(c_kv[:m_ctx].reshape(B, S, KVL))
        kpe_layers.append(kpe_raw[:m_ctx, :ROPE_DIM].reshape(B, S, ROPE_DIM))

    y_prompt = _final_norm(x, g_final, 0, m_ctx).reshape(B, S, D)
    y_sample = _final_norm(x, g_final, m_ctx, m_lat).reshape(DB, DS, D)
    return (y_prompt, y_sample, jnp.stack(ckv_layers, axis=1), jnp.stack(kpe_layers, axis=1))
```

```python
import functools

import numpy as np
import jax
import jax.numpy as jnp
from jax.experimental import pallas as pl
from jax.experimental.pallas import tpu as pltpu

N_HEADS = 16
NOPE_DIM = 128
ROPE_DIM = 64
V_DIM = 128
HEAD_PAD = 256
GRID_W = 64
EPS = 1e-6
ROPE_BASE = 10000.0
SM_SCALE = (NOPE_DIM + ROPE_DIM) ** -0.5
Q_SCALE = SM_SCALE * float(np.log2(np.e))
POOL_WINDOWS = (2, 4, 8, 16)
N_FOURIER_GROUPS = 4
MOD_ROWS = 8
LANE = 128

BF16 = jnp.bfloat16
F32 = jnp.float32


def _params(n_grid, vmem_mb):
    return pltpu.CompilerParams(dimension_semantics=("arbitrary",) * n_grid,
                                vmem_limit_bytes=vmem_mb * 2 ** 20)


def _dot(a, b):
    return jnp.dot(a, b, preferred_element_type=F32)


def _dot_nt(a, b):
    return jax.lax.dot_general(a, b, (((1,), (1,)), ((), ())), preferred_element_type=F32)


SLAB_ROWS = 256


def _slabs(n_rows, slab_rows=SLAB_ROWS):
    r = int(np.gcd(slab_rows, n_rows))
    return [slice(s * r, (s + 1) * r) for s in range(n_rows // r)]


def _mod_row(i, bm, m_ctx, tch):
    start = i * bm
    return jnp.where(start < m_ctx, 0, 1 + (start - m_ctx) // tch)


def _ada_kernel(c_ref, w_ref, b_ref, o_ref):
    c = c_ref[...]
    a = (c * jax.nn.sigmoid(c)).astype(BF16)
    o_ref[0] = _dot(a, w_ref[0].astype(BF16)) + b_ref[0]


def _ada_mod(cond, w_ada, b_ada3, layer):
    _, D, N = w_ada.shape
    bn = 512
    return pl.pallas_call(
        _ada_kernel,
        grid=(N // bn,),
        in_specs=[pl.BlockSpec((MOD_ROWS, D), lambda j: (0, 0)),
                  pl.BlockSpec((1, D, bn), lambda j: (layer, 0, j)),
                  pl.BlockSpec((1, 1, bn), lambda j: (layer, 0, j))],
        out_specs=pl.BlockSpec((1, MOD_ROWS, bn), lambda j: (0, 0, j)),
        out_shape=jax.ShapeDtypeStruct((1, MOD_ROWS, N), F32),
        compiler_params=_params(1, 40),
        name="ada_mod",
    )(cond, w_ada, b_ada3)


def _ada_side_specs(cond, w_ada, b_ada3, layer, n_steps, step_of):
    _, D, N = w_ada.shape
    bn = min(t for t in range(2 * LANE, N + 1, 2 * LANE) if N % t == 0 and N // t <= n_steps)
    n_side = N // bn
    blk = lambda *g: jnp.minimum(step_of(*g), n_side - 1)
    in_specs = [pl.BlockSpec((MOD_ROWS, D), lambda *g: (0, 0)),
                pl.BlockSpec((1, D, bn), lambda *g: (layer, 0, blk(*g))),
                pl.BlockSpec((1, 1, bn), lambda *g: (layer, 0, blk(*g)))]
    out_spec = pl.BlockSpec((1, MOD_ROWS, bn), lambda *g: (0, 0, blk(*g)))
    out_shape = jax.ShapeDtypeStruct((1, MOD_ROWS, N), F32)
    return n_side, in_specs, out_spec, out_shape


def _row_specs(xs, bm, bn, n_grid):
    idx = (lambda *g: (g[-1], g[0])) if n_grid == 2 else (lambda *g: (g[-1], 0))
    if len(xs) == 1:
        return [pl.BlockSpec((bm, bn), idx)]
    n_a = xs[0].shape[0] // bm
    last_b = xs[1].shape[0] // bm - 1
    spec_a = pl.BlockSpec((bm, bn), lambda *g: (jnp.minimum(idx(*g)[0], n_a - 1), idx(*g)[1]))
    spec_b = pl.BlockSpec((bm, bn), lambda *g: (jnp.clip(idx(*g)[0] - n_a, 0, last_b), idx(*g)[1]))
    return [spec_a, spec_b]


def _norm_mod_kernel(*refs, n_x, n_a):
    x_refs, (g_ref, sh_ref, sc_ref, o_ref) = refs[:n_x], refs[n_x:]
    x = x_refs[0][...] if n_x == 1 else jnp.where(pl.program_id(0) < n_a, x_refs[0][...], x_refs[1][...])
    r = jax.lax.rsqrt(jnp.mean(x * x, axis=-1, keepdims=True) + EPS)
    y = x * r * g_ref[...]
    o_ref[...] = (y * (1.0 + sc_ref[0, 0]) + sh_ref[0, 0]).astype(o_ref.dtype)


def _norm_mod(xs, g, mod4, shift_blk, dims):
    M = sum(x.shape[0] for x in xs)
    D = xs[0].shape[1]
    bm = min(512, dims["tch"])
    row = functools.partial(_mod_row, bm=bm, m_ctx=dims["m_ctx"], tch=dims["tch"])
    return pl.pallas_call(
        functools.partial(_norm_mod_kernel, n_x=len(xs), n_a=xs[0].shape[0] // bm),
        grid=(M // bm,),
        in_specs=_row_specs(xs, bm, D, 1) + [
            pl.BlockSpec((1, D), lambda i: (0, 0)),
            pl.BlockSpec((1, 1, 1, D), lambda i: (0, row(i), 0, shift_blk)),
            pl.BlockSpec((1, 1, 1, D), lambda i: (0, row(i), 0, shift_blk + 1))],
        out_specs=pl.BlockSpec((bm, D), lambda i: (i, 0)),
        out_shape=jax.ShapeDtypeStruct((M, D), BF16),
        compiler_params=_params(1, 56),
        name="norm_mod",
    )(*xs, g, mod4, mod4)


def _final_norm_kernel(x_ref, g_ref, o_ref):
    x = x_ref[...]
    r = jax.lax.rsqrt(jnp.mean(x * x, axis=-1, keepdims=True) + EPS)
    o_ref[...] = x * r * g_ref[...]


def _final_norm(x, g, row0, rows):
    D = x.shape[1]
    bm = int(np.gcd(512, np.gcd(rows, row0 if row0 else rows)))
    blk0 = row0 // bm
    return pl.pallas_call(
        _final_norm_kernel,
        grid=(rows // bm,),
        in_specs=[pl.BlockSpec((bm, D), lambda i: (blk0 + i, 0)),
                  pl.BlockSpec((1, D), lambda i: (0, 0))],
        out_specs=pl.BlockSpec((bm, D), lambda i: (i, 0)),
        out_shape=jax.ShapeDtypeStruct((rows, D), F32),
        compiler_params=_params(1, 48),
        name="final_norm",
    )(x, g.reshape(1, D))


IN_BN = 512


def _in_proj_kernel(h_ref, w_ref, wk_ref, z_ref, kpe_ref, g_ref, *, n_small_blk):
    j = pl.program_id(1)

    @pl.when(j < n_small_blk)
    def _():
        w = w_ref[0].astype(BF16)
        for rows in _slabs(h_ref.shape[0]):
            z_ref[rows, :] = _dot_nt(h_ref[rows, :], w)

    @pl.when(j == n_small_blk)
    def _():
        kpe_ref[...] = _dot_nt(h_ref[...], wk_ref[0].astype(BF16))

    @pl.when(j >= n_small_blk)
    def _():
        w = w_ref[0].astype(BF16)
        for rows in _slabs(h_ref.shape[0]):
            g_ref[rows, :] = jax.nn.sigmoid(_dot_nt(h_ref[rows, :], w)).astype(g_ref.dtype)


def _in_proj(h, w_t, layer, off_pe, n_small, bm):
    M, K = h.shape
    N = w_t.shape[1]
    bm = min(bm, M)
    n_small_blk = off_pe // IN_BN
    n_gate_blk = (N - n_small) // IN_BN
    assert off_pe % IN_BN == 0 and (N - n_small) % IN_BN == 0 and n_small % 8 == 0 and off_pe % LANE == 0

    def w_row(j):
        row = jnp.where(j < n_small_blk, j * IN_BN, n_small + (j - n_small_blk) * IN_BN)
        return pl.multiple_of(row, 8)

    return pl.pallas_call(
        functools.partial(_in_proj_kernel, n_small_blk=n_small_blk),
        grid=(M // bm, n_small_blk + n_gate_blk),
        in_specs=[pl.BlockSpec((bm, K), lambda i, j: (i, 0), pipeline_mode=pl.Buffered(1)),
                  pl.BlockSpec((pl.Element(1), pl.Element(IN_BN), pl.Element(K)), lambda i, j: (layer, w_row(j), 0)),
                  pl.BlockSpec((1, LANE, K), lambda i, j: (layer, off_pe // LANE, 0), pipeline_mode=pl.Buffered(1))],
        out_specs=[pl.BlockSpec((bm, IN_BN), lambda i, j: (i, jnp.minimum(j, n_small_blk - 1))),
                   pl.BlockSpec((bm, LANE), lambda i, j: (i, 0)),
                   pl.BlockSpec((bm, IN_BN), lambda i, j: (i, jnp.maximum(j - n_small_blk, 0)))],
        out_shape=[jax.ShapeDtypeStruct((M, off_pe), F32),
                   jax.ShapeDtypeStruct((M, LANE), F32),
                   jax.ShapeDtypeStruct((M, n_gate_blk * IN_BN), BF16)],
        compiler_params=_params(2, 60),
        name="in_proj",
    )(h, w_t, w_t)


def _mm_resid_kernel(a_ref, w_ref, *refs, n_x, n_a, n_side):
    x_refs, rest = refs[:n_x], refs[n_x:]
    gate_ref, o_ref = rest[0], rest[-2 if n_side else -1]
    first = pl.program_id(1) < n_a
    for rows in _slabs(a_ref.shape[0]):
        x = x_refs[0][rows, :] if n_x == 1 else jnp.where(first, x_refs[0][rows, :], x_refs[1][rows, :])
        o_ref[rows, :] = x + gate_ref[0, 0] * _dot(a_ref[rows, :], w_ref[0])

    if n_side:
        c_ref, wa_ref, ba_ref, mod_o = rest[1], rest[2], rest[3], rest[-1]

        @pl.when(pl.program_id(0) * pl.num_programs(1) + pl.program_id(1) < n_side)
        def _():
            _ada_kernel(c_ref, wa_ref, ba_ref, mod_o)


def _mm_resid(a, w, xs, mod4, layer, gate_blk, bm, bn, dims, vmem_mb, name, ada_side=None):
    M, K = a.shape
    N = w.shape[2]
    bm = min(bm, dims["tch"])
    row = functools.partial(_mod_row, bm=bm, m_ctx=dims["m_ctx"], tch=dims["tch"])
    nb, ni = N // bn, M // bm
    in_specs = [pl.BlockSpec((bm, K), lambda j, i: (i, 0)),
                pl.BlockSpec((1, K, bn), lambda j, i: (layer, 0, j))] + _row_specs(xs, bm, bn, 2) + [
                pl.BlockSpec((1, 1, 1, bn), lambda j, i: (0, row(i), 0, gate_blk * nb + j))]
    out_specs = [pl.BlockSpec((bm, bn), lambda j, i: (i, j))]
    out_shape = [jax.ShapeDtypeStruct((M, N), F32)]
    args = [a, w, *xs, mod4]
    n_side = 0
    if ada_side is not None:
        n_side, side_in, side_out, side_shape = _ada_side_specs(*ada_side, nb * ni, lambda j, i: j * ni + i)
        in_specs += side_in
        out_specs.append(side_out)
        out_shape.append(side_shape)
        args += list(ada_side[:3])
    res = pl.pallas_call(
        functools.partial(_mm_resid_kernel, n_x=len(xs), n_a=xs[0].shape[0] // bm, n_side=n_side),
        grid=(nb, ni),
        in_specs=in_specs,
        out_specs=out_specs,
        out_shape=out_shape,
        compiler_params=_params(2, vmem_mb),
        name=name,
    )(*args)
    return res if n_side else res[0]


def _swiglu_kernel(a_ref, wg_ref, wu_ref, wd_ref, o_ref, wd_o, *, n_cast):
    wg = wg_ref[0].astype(BF16)
    wu = wu_ref[0].astype(BF16)
    for rows in _slabs(a_ref.shape[0]):
        a = a_ref[rows, :]
        g = _dot(a, wg)
        u = _dot(a, wu)
        o_ref[rows, :] = (g * jax.nn.sigmoid(g) * u).astype(o_ref.dtype)

    @pl.when(pl.program_id(0) * pl.num_programs(1) + pl.program_id(1) < n_cast)
    def _():
        wd_o[...] = wd_ref[0].astype(wd_o.dtype)


def _swiglu_up(a, wg, wu, wd, layer, bm, bn):
    M, K = a.shape
    N = wg.shape[2]
    KD, ND = wd.shape[1:]
    bm = min(bm, M)
    nj = N // bn
    n_steps = (M // bm) * nj
    n_cast = max(n for n in range(1, n_steps + 1) if KD % n == 0 and (KD // n) % 16 == 0)
    rb = KD // n_cast
    cast_blk = lambda i, j: jnp.minimum(i * nj + j, n_cast - 1)
    return pl.pallas_call(
        functools.partial(_swiglu_kernel, n_cast=n_cast),
        grid=(M // bm, nj),
        in_specs=[pl.BlockSpec((bm, K), lambda i, j: (i, 0)),
                  pl.BlockSpec((1, K, bn), lambda i, j: (layer, 0, j)),
                  pl.BlockSpec((1, K, bn), lambda i, j: (layer, 0, j)),
                  pl.BlockSpec((1, rb, ND), lambda i, j: (layer, cast_blk(i, j), 0))],
        out_specs=[pl.BlockSpec((bm, bn), lambda i, j: (i, j)),
                   pl.BlockSpec((rb, ND), lambda i, j: (cast_blk(i, j), 0))],
        out_shape=[jax.ShapeDtypeStruct((M, N), BF16),
                   jax.ShapeDtypeStruct((KD, ND), BF16)],
        compiler_params=_params(2, 60),
        name="swiglu_up",
    )(a, wg, wu, wd)


def _merge_kernel(af_ref, ap_ref, aoc_ref, aol_ref, wf_ref, wp_ref, wo_ref, g0_ref, g1_ref, g2_ref, o_ref, *, n_a):
    first = pl.program_id(1) < n_a
    for rows in _slabs(af_ref.shape[0]):
        ao = jnp.where(first, aoc_ref[rows, :], aol_ref[rows, :])
        y = g0_ref[rows, :].astype(F32) * _dot(af_ref[rows, :], wf_ref[0])
        y = y + g1_ref[rows, :].astype(F32) * _dot(ap_ref[rows, :], wp_ref[0])
        y = y + g2_ref[rows, :].astype(F32) * _dot(ao, wo_ref[0])
        o_ref[rows, :] = y.astype(o_ref.dtype)


def _merge(af, ap, aos, wf, wp, wo, gates, layer, bm, bn):
    M = af.shape[0]
    N = wf.shape[2]
    bm = min(bm, aos[0].shape[0], aos[1].shape[0])
    nb = N // bn
    a_spec = lambda a: pl.BlockSpec((bm, a.shape[1]), lambda j, i: (i, 0))
    w_spec = lambda w: pl.BlockSpec((1, w.shape[1], bn), lambda j, i: (layer, 0, j))
    g_spec = lambda k: pl.BlockSpec((bm, bn), lambda j, i: (i, k * nb + j))
    return pl.pallas_call(
        functools.partial(_merge_kernel, n_a=aos[0].shape[0] // bm),
        grid=(nb, M // bm),
        in_specs=[a_spec(af), a_spec(ap)] + _row_specs(aos, bm, aos[0].shape[1], 1) + [
                  w_spec(wf), w_spec(wp), w_spec(wo), g_spec(0), g_spec(1), g_spec(2)],
        out_specs=pl.BlockSpec((bm, bn), lambda j, i: (i, j)),
        out_shape=jax.ShapeDtypeStruct((M, N), BF16),
        compiler_params=_params(2, 60),
        name="merge",
    )(af, ap, *aos, wf, wp, wo, gates, gates, gates)


def _dft_group(f_ref, ch_ref, pos_ref, o_ref, g, s, fg):
    tch = f_ref.shape[0]
    scale = 1.0 / float(np.sqrt(s * fg))
    x = f_ref[:, g * fg:(g + 1) * fg].astype(BF16)
    xcs = _dot(x, ch_ref[...]).astype(BF16)
    for q in range(tch // s):
        rows = slice(q * s, (q + 1) * s)
        stacked = jnp.concatenate([xcs[rows, :fg], xcs[rows, fg:]], axis=0)
        y = _dot(pos_ref[...], stacked) * scale
        o_ref[rows, g * fg:(g + 1) * fg] = y.astype(o_ref.dtype)


def _pool_group(p_ref, ps_ref, o_ref, gi, s, pg):
    tch = p_ref.shape[0]
    w = POOL_WINDOWS[gi]
    half = w // 2
    assert w == 2 * half and half & (half - 1) == 0
    pos = jax.lax.broadcasted_iota(jnp.int32, (tch, pg), 0) & (s - 1)
    cols = slice(gi * pg, (gi + 1) * pg)
    x = p_ref[:, cols]
    before = jnp.where(pos >= 1, pltpu.roll(x, 1, 0), 0.0)
    after = x
    m = 1
    while m < half:
        before = before + jnp.where(pos >= m, pltpu.roll(before, m, 0), 0.0)
        after = after + jnp.where(pos + m < s, pltpu.roll(after, tch - m, 0), 0.0)
        m *= 2
    cnt = (jnp.minimum(pos + half, s) - jnp.maximum(pos - half, 0)).astype(F32)
    o_ref[:, cols] = (((before + after) / cnt - x) * ps_ref[:, cols]).astype(o_ref.dtype)


def _dft_kernel(f_ref, ch_ref, pc_ref, pl_ref, o_ref, *, n_ctx_chunks, s_ctx, fg):
    i = pl.program_id(0)

    @pl.when(i < n_ctx_chunks)
    def _():
        for g in range(N_FOURIER_GROUPS):
            _dft_group(f_ref, ch_ref, pc_ref, o_ref, g, s_ctx, fg)

    @pl.when(i >= n_ctx_chunks)
    def _():
        for g in range(N_FOURIER_GROUPS):
            _dft_group(f_ref, ch_ref, pl_ref, o_ref, g, f_ref.shape[0], fg)


def _pool_kernel(p_ref, ps_ref, o_ref, *, n_ctx_chunks, s_ctx, pg):
    i = pl.program_id(0)

    @pl.when(i < n_ctx_chunks)
    def _():
        for gi in range(len(POOL_WINDOWS)):
            _pool_group(p_ref, ps_ref, o_ref, gi, s_ctx, pg)

    @pl.when(i >= n_ctx_chunks)
    def _():
        for gi in range(len(POOL_WINDOWS)):
            _pool_group(p_ref, ps_ref, o_ref, gi, p_ref.shape[0], pg)


def _dft_tables(s_ctx, s_lat, fg):
    def cs(n):
        k = np.arange(n, dtype=np.int64)
        ang = 2.0 * np.pi * ((k[:, None] * k[None, :]) % n).astype(np.float64) / n
        return np.cos(ang), np.sin(ang)
    cc, sc = cs(fg)
    chan = np.concatenate([cc, sc], axis=1)
    pos = []
    for s in (s_ctx, s_lat):
        c, sn = cs(s)
        pos.append(np.concatenate([c, -sn], axis=1))
    return (jnp.asarray(chan, BF16), jnp.asarray(pos[0], BF16), jnp.asarray(pos[1], BF16))


def _dft(z, dims, df):
    M = z.shape[0]
    tch, s_ctx = dims["tch"], dims["s_ctx"]
    fg = df // N_FOURIER_GROUPS
    chan, pos_c, pos_l = _dft_tables(s_ctx, tch, fg)
    full = lambda a: pl.BlockSpec(a.shape, lambda i: (0, 0))
    return pl.pallas_call(
        functools.partial(_dft_kernel, n_ctx_chunks=dims["m_ctx"] // tch, s_ctx=s_ctx, fg=fg),
        grid=(M // tch,),
        in_specs=[pl.BlockSpec((tch, df), lambda i: (i, 0)), full(chan), full(pos_c), full(pos_l)],
        out_specs=pl.BlockSpec((tch, df), lambda i: (i, 0)),
        out_shape=jax.ShapeDtypeStruct((M, df), BF16),
        compiler_params=_params(1, 48),
        name="dft_mix",
    )(z, chan, pos_c, pos_l)


def _pool(z, pool_scale_l, dims, dp, col_blk):
    M = z.shape[0]
    tch = dims["tch"]
    return pl.pallas_call(
        functools.partial(_pool_kernel, n_ctx_chunks=dims["m_ctx"] // tch, s_ctx=dims["s_ctx"],
                          pg=dp // len(POOL_WINDOWS)),
        grid=(M // tch,),
        in_specs=[pl.BlockSpec((tch, dp), lambda i: (i, col_blk)),
                  pl.BlockSpec((1, dp), lambda i: (0, 0))],
        out_specs=pl.BlockSpec((tch, dp), lambda i: (i, 0)),
        out_shape=jax.ShapeDtypeStruct((M, dp), BF16),
        compiler_params=_params(1, 32),
        name="pool",
    )(z, pool_scale_l.reshape(1, dp))


def _rope_tables(n_tokens):
    rows = n_tokens // GRID_W
    row = np.repeat(np.arange(rows, dtype=np.float64), GRID_W)
    col = np.tile(np.arange(GRID_W, dtype=np.float64), rows)
    inv = ROPE_BASE ** (-np.arange(ROPE_DIM // 4, dtype=np.float64) * (4.0 / ROPE_DIM))
    ar = row[:, None] * inv[None, :]
    ac = col[:, None] * inv[None, :]
    ang = np.concatenate([ar, ar, ac, ac], axis=1)
    sign = np.tile(np.concatenate([-np.ones(ROPE_DIM // 4), np.ones(ROPE_DIM // 4)]), 2)
    cos = np.concatenate([np.cos(ang), np.ones_like(ang)], axis=1)
    sin = np.concatenate([np.sin(ang) * sign[None, :], np.zeros_like(ang)], axis=1)
    return jnp.asarray(cos, F32), jnp.asarray(sin, F32)


def _rope(x, cos, sin):
    q = ROPE_DIM // 4
    lane = jax.lax.broadcasted_iota(jnp.int32, x.shape, 1)
    first = (lane & (2 * q - 1)) < q
    swapped = jnp.where(first, pltpu.roll(x, LANE - q, 1), pltpu.roll(x, q, 1))
    return x * cos + swapped * sin


def _q_kernel(cq_ref, g_ref, w_ref, cos_ref, sin_ref, o_ref, wp_ref, *, n_ctx_tiles):
    i = pl.program_id(0)

    @pl.when(i == 0)
    def _():
        hd = NOPE_DIM + ROPE_DIM
        for h in range(N_HEADS):
            wp_ref[:, h * HEAD_PAD:h * HEAD_PAD + hd] = w_ref[0, :, h * hd:(h + 1) * hd].astype(BF16)
            wp_ref[:, h * HEAD_PAD + hd:(h + 1) * HEAD_PAD] = jnp.zeros((wp_ref.shape[0], HEAD_PAD - hd), BF16)

    def q_rows(rows):
        x = cq_ref[rows, :]
        r = jax.lax.rsqrt(jnp.mean(x * x, axis=-1, keepdims=True) + EPS)
        return _dot((x * r * g_ref[...]).astype(BF16), wp_ref[...]) * Q_SCALE

    @pl.when(i < n_ctx_tiles)
    def _():
        for rows in _slabs(cq_ref.shape[0]):
            o_ref[rows, :] = q_rows(rows).astype(o_ref.dtype)

    @pl.when(i >= n_ctx_tiles)
    def _():
        for rows in _slabs(cq_ref.shape[0]):
            q = q_rows(rows)
            cos, sin = cos_ref[rows, :], sin_ref[rows, :]
            for h in range(N_HEADS):
                lo = h * HEAD_PAD
                o_ref[rows, lo:lo + NOPE_DIM] = q[:, lo:lo + NOPE_DIM].astype(o_ref.dtype)
                pe = q[:, lo + NOPE_DIM:lo + HEAD_PAD]
                o_ref[rows, lo + NOPE_DIM:lo + HEAD_PAD] = _rope(pe, cos, sin).astype(o_ref.dtype)


def _q_proj(z, g_q_l, w_q_b, layer, cos, sin, dims, ql, col_blk):
    M = z.shape[0]
    N = N_HEADS * HEAD_PAD
    bm = min(512, dims["tch"])
    n_ctx_tiles = dims["m_ctx"] // bm
    per_seq = dims["tch"] // bm
    pos_blk = lambda i: (jnp.maximum(i - n_ctx_tiles, 0) % per_seq, 0)
    return pl.pallas_call(
        functools.partial(_q_kernel, n_ctx_tiles=n_ctx_tiles),
        grid=(M // bm,),
        in_specs=[pl.BlockSpec((bm, ql), lambda i: (i, col_blk)),
                  pl.BlockSpec((1, ql), lambda i: (0, 0)),
                  pl.BlockSpec((1,) + w_q_b.shape[1:], lambda i: (layer, 0, 0), pipeline_mode=pl.Buffered(1)),
                  pl.BlockSpec((bm, LANE), pos_blk),
                  pl.BlockSpec((bm, LANE), pos_blk)],
        out_specs=pl.BlockSpec((bm, N), lambda i: (i, 0)),
        out_shape=jax.ShapeDtypeStruct((M, N), BF16),
        scratch_shapes=[pltpu.VMEM((ql, N), BF16)],
        compiler_params=_params(1, 56),
        name="q_proj",
    )(z, g_q_l.reshape(1, ql), w_q_b, cos, sin)


def _kv_kernel(ckv_ref, g_ref, w_ref, kpe_ref, cos_ref, sin_ref, ckv_o, kv_o, kpe_o, *, n_ctx_tiles):
    i = pl.program_id(0)
    for rows in _slabs(ckv_ref.shape[0]):
        x = ckv_ref[rows, :]
        r = jax.lax.rsqrt(jnp.mean(x * x, axis=-1, keepdims=True) + EPS)
        c = x * r * g_ref[...]
        ckv_o[rows, :] = c
        kv_o[rows, :] = _dot(c.astype(BF16), w_ref[...]).astype(kv_o.dtype)
    kpe = kpe_ref[...]
    kpe = jnp.where(jax.lax.broadcasted_iota(jnp.int32, kpe.shape, 1) < ROPE_DIM, kpe, 0.0)

    @pl.when(i < n_ctx_tiles)
    def _():
        kpe_o[...] = kpe.astype(kpe_o.dtype)

    @pl.when(i >= n_ctx_tiles)
    def _():
        kpe_o[...] = _rope(kpe, cos_ref[...], sin_ref[...]).astype(kpe_o.dtype)


def _kv_proj(z, kpe_raw, g_kv_l, w_kv, cos, sin, dims, kvl, ckv_blk):
    M = z.shape[0]
    N = w_kv.shape[1]
    bm = min(512, dims["tch"])
    n_ctx_tiles = dims["m_ctx"] // bm
    per_seq = dims["tch"] // bm
    pos_blk = lambda i: (jnp.maximum(i - n_ctx_tiles, 0) % per_seq, 0)
    return pl.pallas_call(
        functools.partial(_kv_kernel, n_ctx_tiles=n_ctx_tiles),
        grid=(M // bm,),
        in_specs=[pl.BlockSpec((bm, kvl), lambda i: (i, ckv_blk)),
                  pl.BlockSpec((1, kvl), lambda i: (0, 0)),
                  pl.BlockSpec((kvl, N), lambda i: (0, 0)),
                  pl.BlockSpec((bm, LANE), lambda i: (i, 0)),
                  pl.BlockSpec((bm, LANE), pos_blk),
                  pl.BlockSpec((bm, LANE), pos_blk)],
        out_specs=[pl.BlockSpec((bm, kvl), lambda i: (i, 0)),
                   pl.BlockSpec((bm, N), lambda i: (i, 0)),
                   pl.BlockSpec((bm, LANE), lambda i: (i, 0))],
        out_shape=[jax.ShapeDtypeStruct((M, kvl), F32),
                   jax.ShapeDtypeStruct((M, N), BF16),
                   jax.ShapeDtypeStruct((M, LANE), BF16)],
        compiler_params=_params(1, 48),
        name="kv_proj",
    )(z, g_kv_l.reshape(1, kvl), w_kv, kpe_raw, cos, sin)


def _kv_cache_kernel(c_ref, w_ref, o_ref):
    o_ref[...] = _dot(c_ref[0, 0].astype(BF16), w_ref[...]).astype(o_ref.dtype)


def _kv_cache_proj(cache_ckv, w_kv, layer):
    DB, _, P, kvl = cache_ckv.shape
    N = w_kv.shape[1]
    return pl.pallas_call(
        _kv_cache_kernel,
        grid=(DB,),
        in_specs=[pl.BlockSpec((1, 1, P, kvl), lambda b: (b, layer, 0, 0)),
                  pl.BlockSpec((kvl, N), lambda b: (0, 0))],
        out_specs=pl.BlockSpec((P, N), lambda b: (b, 0)),
        out_shape=jax.ShapeDtypeStruct((DB * P, N), BF16),
        compiler_params=_params(1, 32),
        name="kv_cache_proj",
    )(cache_ckv, w_kv)


def _attend(qh, keys, values):
    ss = [_dot_nt(qh, k) for k in keys]
    m = functools.reduce(jnp.maximum, [jnp.max(s, axis=-1, keepdims=True) for s in ss])
    acc = None
    for s, v in zip(ss, values):
        p = jnp.exp2(s - m).astype(BF16)
        part = _dot(p, jnp.concatenate([v, jnp.ones_like(v)], axis=1))
        acc = part if acc is None else acc + part
    return acc[:, :V_DIM] * (1.0 / acc[:, V_DIM:])


def _attn_ctx_kernel(q_ref, kv_ref, kpe_ref, o_ref):
    kpe = kpe_ref[...]
    for h in range(N_HEADS):
        lo = h * HEAD_PAD
        k = jnp.concatenate([kv_ref[:, lo:lo + NOPE_DIM], kpe], axis=1)
        o = _attend(q_ref[:, lo:lo + HEAD_PAD], [k], [kv_ref[:, lo + NOPE_DIM:lo + HEAD_PAD]])
        o_ref[:, h * V_DIM:(h + 1) * V_DIM] = o.astype(o_ref.dtype)


def _attn_ctx(q, kv, kpe, dims):
    M = q.shape[0]
    s = dims["s_ctx"]
    n = dims["m_ctx"] // s
    dv = N_HEADS * V_DIM
    return pl.pallas_call(
        _attn_ctx_kernel,
        grid=(n,),
        in_specs=[pl.BlockSpec((s, q.shape[1]), lambda b: (b, 0)),
                  pl.BlockSpec((s, kv.shape[1]), lambda b: (b, 0)),
                  pl.BlockSpec((s, LANE), lambda b: (b, 0))],
        out_specs=pl.BlockSpec((s, dv), lambda b: (b, 0)),
        out_shape=jax.ShapeDtypeStruct((dims["m_ctx"], dv), BF16),
        compiler_params=_params(1, 32),
        name="attn_ctx",
    )(q, kv, kpe)


def _attn_lat_kernel(q_ref, kv_ref, kpe_ref, kvc_ref, kpec_ref, o_ref):
    kpe = kpe_ref[...]
    kpec = kpec_ref[0, 0].astype(BF16)
    kpec = jnp.concatenate([kpec, jnp.zeros_like(kpec)], axis=1)
    for h in range(N_HEADS):
        lo = h * HEAD_PAD
        qh = q_ref[:, lo:lo + HEAD_PAD]
        k_own = jnp.concatenate([kv_ref[:, lo:lo + NOPE_DIM], kpe], axis=1)
        k_c = jnp.concatenate([kvc_ref[:, lo:lo + NOPE_DIM], kpec], axis=1)
        o = _attend(qh, [k_c, k_own], [kvc_ref[:, lo + NOPE_DIM:lo + HEAD_PAD], kv_ref[:, lo + NOPE_DIM:lo + HEAD_PAD]])
        o_ref[:, h * V_DIM:(h + 1) * V_DIM] = o.astype(o_ref.dtype)


def _attn_lat(q, kv, kpe, kv_cache, cache_krope, layer, dims):
    tch, m_ctx = dims["tch"], dims["m_ctx"]
    DB, _, P, rd = cache_krope.shape
    bq = min(256, tch)
    nq = tch // bq
    dv = N_HEADS * V_DIM
    q_blk = lambda b, t: (m_ctx // bq + b * nq + t, 0)
    own_blk = lambda b, t: (m_ctx // tch + b, 0)
    return pl.pallas_call(
        _attn_lat_kernel,
        grid=(DB, nq),
        in_specs=[pl.BlockSpec((bq, q.shape[1]), q_blk),
                  pl.BlockSpec((tch, kv.shape[1]), own_blk),
                  pl.BlockSpec((tch, LANE), own_blk),
                  pl.BlockSpec((P, kv_cache.shape[1]), lambda b, t: (b, 0)),
                  pl.BlockSpec((1, 1, P, rd), lambda b, t: (b, layer, 0, 0))],
        out_specs=pl.BlockSpec((bq, dv), lambda b, t: (b * nq + t, 0)),
        out_shape=jax.ShapeDtypeStruct((DB * tch, dv), BF16),
        compiler_params=_params(2, 48),
        name="attn_lat",
    )(q, kv, kpe, kv_cache, cache_krope)


def kernel(x_prompt, x_sample, cache_ckv, cache_krope, c, c_ctx, w_ada, b_ada, g_norm1, g_norm2, w_in, w_fourier,
           pool_scale, w_pool, g_q, w_q_b, g_kv, w_kv_b, w_o_mla, w_out, w_gate, w_up, w_down, g_final):
    B, S, D = x_prompt.shape
    DB, DS, _ = x_sample.shape
    L = w_ada.shape[0]
    DF = w_fourier.shape[1]
    DP = pool_scale.shape[1]
    QL = g_q.shape[1]
    KVL = g_kv.shape[1]
    m_ctx, m_lat = B * S, DB * DS
    M = m_ctx + m_lat
    assert DS % S == 0 and m_ctx % DS == 0 and S & (S - 1) == 0 and DS & (DS - 1) == 0
    assert 1 + DB <= MOD_ROWS and DS % GRID_W == 0
    dims = dict(m_ctx=m_ctx, tch=DS, s_ctx=S)

    off_p, off_q, off_kv, off_pe = DF, DF + DP, DF + DP + QL, DF + DP + QL + KVL
    assert off_p % DP == 0 and off_q % QL == 0 and off_kv % KVL == 0 and off_pe % IN_BN == 0

    xs = (x_prompt.reshape(m_ctx, D), x_sample.reshape(m_lat, D))
    cond = jnp.zeros((MOD_ROWS, D), F32).at[0].set(c_ctx).at[1:1 + DB].set(c)
    b_ada3 = b_ada.reshape(L, 1, 6 * D)
    mod4 = _ada_mod(cond, w_ada, b_ada3, 0).reshape(1, MOD_ROWS, 1, 6 * D)
    cos, sin = _rope_tables(DS)
    w_in_t = jnp.swapaxes(w_in, 1, 2)
    wf_b, wp_b, wo_b = w_fourier.astype(BF16), w_pool.reshape(L, DP, D).astype(BF16), w_o_mla.astype(BF16)
    w_out_b = w_out.astype(BF16)

    ckv_layers, kpe_layers = [], []
    for l in range(L):
        w_kv = w_kv_b[l].astype(BF16)

        h = _norm_mod(xs, g_norm1[l:l + 1], mod4, 0, dims)
        z, kpe_raw, gates = _in_proj(h, w_in_t, l, off_pe, off_pe + ROPE_DIM, 2048)

        mixed = _dft(z, dims, DF)
        pooled = _pool(z, pool_scale[l], dims, DP, off_p // DP)
        q = _q_proj(z, g_q[l], w_q_b, l, cos, sin, dims, QL, off_q // QL)
        c_kv, kv, kpe = _kv_proj(z, kpe_raw, g_kv[l], w_kv, cos, sin, dims, KVL, off_kv // KVL)
        kv_cache = _kv_cache_proj(cache_ckv, w_kv, l)
        o_ctx = _attn_ctx(q, kv, kpe, dims)
        o_lat = _attn_lat(q, kv, kpe, kv_cache, cache_krope, l, dims)

        merged = _merge(mixed, pooled, (o_ctx, o_lat), wf_b, wp_b, wo_b, gates, l, 1024, 1024)
        x = _mm_resid(merged, w_out_b, xs, mod4, l, 2, 1024, 1024, dims, 60, "out_proj")

        h = _norm_mod((x,), g_norm2[l:l + 1], mod4, 3, dims)
        act, w_down_b = _swiglu_up(h, w_gate, w_up, w_down, l, 2048, 256)
        if l + 1 < L:
            x, mod_next = _mm_resid(act, w_down_b[None], (x,), mod4, 0, 5, 512, 512, dims, 60, "down_proj",
                                    ada_side=(cond, w_ada, b_ada3, l + 1))
            mod4 = mod_next.reshape(1, MOD_ROWS, 1, 6 * D)
        else:
            x = _mm_resid(act, w_down_b[None], (x,), mod4, 0, 5, 512, 512, dims, 56, "down_proj")
        xs = (x,)

        ckv_layers.append(c_kv[:m_ctx].reshape(B, S, KVL))
        kpe_layers.append(kpe_raw[:m_ctx, :ROPE_DIM].reshape(B, S, ROPE_DIM))

    y_prompt = _final_norm(x, g_final, 0, m_ctx).reshape(B, S, D)
    y_sample = _final_norm(x, g_final, m_ctx, m_lat).reshape(DB, DS, D)
    return (y_prompt, y_sample, jnp.stack(ckv_layers, axis=1), jnp.stack(kpe_layers, axis=1))
```

```python
import functools

import numpy as np
import jax
import jax.numpy as jnp
from jax.experimental import pallas as pl
from jax.experimental.pallas import tpu as pltpu

N_HEADS = 16
NOPE_DIM = 128
ROPE_DIM = 64
V_DIM = 128
HEAD_PAD = 256
GRID_W = 64
EPS = 1e-6
ROPE_BASE = 10000.0
SM_SCALE = (NOPE_DIM + ROPE_DIM) ** -0.5
Q_SCALE = SM_SCALE * float(np.log2(np.e))
POOL_WINDOWS = (2, 4, 8, 16)
N_FOURIER_GROUPS = 4
MOD_ROWS = 8
LANE = 128

BF16 = jnp.bfloat16
F32 = jnp.float32


def _params(n_grid, vmem_mb):
    return pltpu.CompilerParams(dimension_semantics=("arbitrary",) * n_grid,
                                vmem_limit_bytes=vmem_mb * 2 ** 20)


def _dot(a, b):
    return jnp.dot(a, b, preferred_element_type=F32)


def _dot_nt(a, b):
    return jax.lax.dot_general(a, b, (((1,), (1,)), ((), ())), preferred_element_type=F32)


SLAB_ROWS = 256


def _slabs(n_rows, slab_rows=SLAB_ROWS):
    r = int(np.gcd(slab_rows, n_rows))
    return [slice(s * r, (s + 1) * r) for s in range(n_rows // r)]


def _mod_row(i, bm, m_ctx, tch):
    start = i * bm
    return jnp.where(start < m_ctx, 0, 1 + (start - m_ctx) // tch)


def _ada_kernel(c_ref, w_ref, b_ref, o_ref):
    c = c_ref[...]
    a = (c * jax.nn.sigmoid(c)).astype(BF16)
    o_ref[0] = _dot(a, w_ref[0].astype(BF16)) + b_ref[0]


def _ada_mod(cond, w_ada, b_ada3, layer):
    _, D, N = w_ada.shape
    bn = 512
    return pl.pallas_call(
        _ada_kernel,
        grid=(N // bn,),
        in_specs=[pl.BlockSpec((MOD_ROWS, D), lambda j: (0, 0)),
                  pl.BlockSpec((1, D, bn), lambda j: (layer, 0, j)),
                  pl.BlockSpec((1, 1, bn), lambda j: (layer, 0, j))],
        out_specs=pl.BlockSpec((1, MOD_ROWS, bn), lambda j: (0, 0, j)),
        out_shape=jax.ShapeDtypeStruct((1, MOD_ROWS, N), F32),
        compiler_params=_params(1, 40),
        name="ada_mod",
    )(cond, w_ada, b_ada3)


def _ada_side_specs(cond, w_ada, b_ada3, layer, n_steps, step_of):
    _, D, N = w_ada.shape
    bn = min(t for t in range(2 * LANE, N + 1, 2 * LANE) if N % t == 0 and N // t <= n_steps)
    n_side = N // bn
    blk = lambda *g: jnp.minimum(step_of(*g), n_side - 1)
    in_specs = [pl.BlockSpec((MOD_ROWS, D), lambda *g: (0, 0)),
                pl.BlockSpec((1, D, bn), lambda *g: (layer, 0, blk(*g))),
                pl.BlockSpec((1, 1, bn), lambda *g: (layer, 0, blk(*g)))]
    out_spec = pl.BlockSpec((1, MOD_ROWS, bn), lambda *g: (0, 0, blk(*g)))
    out_shape = jax.ShapeDtypeStruct((1, MOD_ROWS, N), F32)
    return n_side, in_specs, out_spec, out_shape


def _row_specs(xs, bm, bn, n_grid):
    idx = (lambda *g: (g[-1], g[0])) if n_grid == 2 else (lambda *g: (g[-1], 0))
    if len(xs) == 1:
        return [pl.BlockSpec((bm, bn), idx)]
    n_a = xs[0].shape[0] // bm
    last_b = xs[1].shape[0] // bm - 1
    spec_a = pl.BlockSpec((bm, bn), lambda *g: (jnp.minimum(idx(*g)[0], n_a - 1), idx(*g)[1]))
    spec_b = pl.BlockSpec((bm, bn), lambda *g: (jnp.clip(idx(*g)[0] - n_a, 0, last_b), idx(*g)[1]))
    return [spec_a, spec_b]


def _norm_mod_kernel(*refs, n_x, n_a):
    x_refs, (g_ref, sh_ref, sc_ref, o_ref) = refs[:n_x], refs[n_x:]
    x = x_refs[0][...] if n_x == 1 else jnp.where(pl.program_id(0) < n_a, x_refs[0][...], x_refs[1][...])
    r = jax.lax.rsqrt(jnp.mean(x * x, axis=-1, keepdims=True) + EPS)
    y = x * r * g_ref[...]
    o_ref[...] = (y * (1.0 + sc_ref[0, 0]) + sh_ref[0, 0]).astype(o_ref.dtype)


def _norm_mod(xs, g, mod4, shift_blk, dims):
    M = sum(x.shape[0] for x in xs)
    D = xs[0].shape[1]
    bm = min(512, dims["tch"])
    row = functools.partial(_mod_row, bm=bm, m_ctx=dims["m_ctx"], tch=dims["tch"])
    return pl.pallas_call(
        functools.partial(_norm_mod_kernel, n_x=len(xs), n_a=xs[0].shape[0] // bm),
        grid=(M // bm,),
        in_specs=_row_specs(xs, bm, D, 1) + [
            pl.BlockSpec((1, D), lambda i: (0, 0)),
            pl.BlockSpec((1, 1, 1, D), lambda i: (0, row(i), 0, shift_blk)),
            pl.BlockSpec((1, 1, 1, D), lambda i: (0, row(i), 0, shift_blk + 1))],
        out_specs=pl.BlockSpec((bm, D), lambda i: (i, 0)),
        out_shape=jax.ShapeDtypeStruct((M, D), BF16),
        compiler_params=_params(1, 56),
        name="norm_mod",
    )(*xs, g, mod4, mod4)


def _final_norm_kernel(x_ref, g_ref, o_ref):
    x = x_ref[...]
    r = jax.lax.rsqrt(jnp.mean(x * x, axis=-1, keepdims=True) + EPS)
    o_ref[...] = x * r * g_ref[...]


def _final_norm(x, g, row0, rows):
    D = x.shape[1]
    bm = int(np.gcd(512, np.gcd(rows, row0 if row0 else rows)))
    blk0 = row0 // bm
    return pl.pallas_call(
        _final_norm_kernel,
        grid=(rows // bm,),
        in_specs=[pl.BlockSpec((bm, D), lambda i: (blk0 + i, 0)),
                  pl.BlockSpec((1, D), lambda i: (0, 0))],
        out_specs=pl.BlockSpec((bm, D), lambda i: (i, 0)),
        out_shape=jax.ShapeDtypeStruct((rows, D), F32),
        compiler_params=_params(1, 48),
        name="final_norm",
    )(x, g.reshape(1, D))


IN_BN = 512


def _in_proj_kernel(h_ref, w_ref, wk_ref, z_ref, kpe_ref, g_ref, *, n_small_blk):
    j = pl.program_id(1)

    @pl.when(j < n_small_blk)
    def _():
        w = w_ref[0].astype(BF16)
        for rows in _slabs(h_ref.shape[0]):
            z_ref[rows, :] = _dot_nt(h_ref[rows, :], w)

    @pl.when(j == n_small_blk)
    def _():
        kpe_ref[...] = _dot_nt(h_ref[...], wk_ref[0].astype(BF16))

    @pl.when(j >= n_small_blk)
    def _():
        w = w_ref[0].astype(BF16)
        for rows in _slabs(h_ref.shape[0]):
            g_ref[rows, :] = jax.nn.sigmoid(_dot_nt(h_ref[rows, :], w)).astype(g_ref.dtype)


def _in_proj(h, w_t, layer, off_pe, n_small, bm):
    M, K = h.shape
    N = w_t.shape[1]
    bm = min(bm, M)
    n_small_blk = off_pe // IN_BN
    n_gate_blk = (N - n_small) // IN_BN
    assert off_pe % IN_BN == 0 and (N - n_small) % IN_BN == 0 and n_small % 8 == 0 and off_pe % LANE == 0

    def w_row(j):
        row = jnp.where(j < n_small_blk, j * IN_BN, n_small + (j - n_small_blk) * IN_BN)
        return pl.multiple_of(row, 8)

    return pl.pallas_call(
        functools.partial(_in_proj_kernel, n_small_blk=n_small_blk),
        grid=(M // bm, n_small_blk + n_gate_blk),
        in_specs=[pl.BlockSpec((bm, K), lambda i, j: (i, 0), pipeline_mode=pl.Buffered(1)),
                  pl.BlockSpec((pl.Element(1), pl.Element(IN_BN), pl.Element(K)), lambda i, j: (layer, w_row(j), 0)),
                  pl.BlockSpec((1, LANE, K), lambda i, j: (layer, off_pe // LANE, 0), pipeline_mode=pl.Buffered(1))],
        out_specs=[pl.BlockSpec((bm, IN_BN), lambda i, j: (i, jnp.minimum(j, n_small_blk - 1))),
                   pl.BlockSpec((bm, LANE), lambda i, j: (i, 0)),
                   pl.BlockSpec((bm, IN_BN), lambda i, j: (i, jnp.maximum(j - n_small_blk, 0)))],
        out_shape=[jax.ShapeDtypeStruct((M, off_pe), F32),
                   jax.ShapeDtypeStruct((M, LANE), F32),
                   jax.ShapeDtypeStruct((M, n_gate_blk * IN_BN), BF16)],
        compiler_params=_params(2, 60),
        name="in_proj",
    )(h, w_t, w_t)


def _mm_resid_kernel(a_ref, w_ref, *refs, n_x, n_a, n_side):
    x_refs, rest = refs[:n_x], refs[n_x:]
    gate_ref, o_ref = rest[0], rest[-2 if n_side else -1]
    def run(x_ref):
        for rows in _slabs(a_ref.shape[0]):
            o_ref[rows, :] = x_ref[rows, :] + gate_ref[0, 0] * _dot(a_ref[rows, :], w_ref[0])

    if n_x == 1:
        run(x_refs[0])
    else:
        pl.when(pl.program_id(1) < n_a)(lambda: run(x_refs[0]))
        pl.when(pl.program_id(1) >= n_a)(lambda: run(x_refs[1]))

    if n_side:
        c_ref, wa_ref, ba_ref, mod_o = rest[1], rest[2], rest[3], rest[-1]

        @pl.when(pl.program_id(0) * pl.num_programs(1) + pl.program_id(1) < n_side)
        def _():
            _ada_kernel(c_ref, wa_ref, ba_ref, mod_o)


def _mm_resid(a, w, xs, mod4, layer, gate_blk, bm, bn, dims, vmem_mb, name, ada_side=None):
    M, K = a.shape
    N = w.shape[2]
    bm = min(bm, dims["tch"])
    row = functools.partial(_mod_row, bm=bm, m_ctx=dims["m_ctx"], tch=dims["tch"])
    nb, ni = N // bn, M // bm
    in_specs = [pl.BlockSpec((bm, K), lambda j, i: (i, 0)),
                pl.BlockSpec((1, K, bn), lambda j, i: (layer, 0, j))] + _row_specs(xs, bm, bn, 2) + [
                pl.BlockSpec((1, 1, 1, bn), lambda j, i: (0, row(i), 0, gate_blk * nb + j))]
    out_specs = [pl.BlockSpec((bm, bn), lambda j, i: (i, j))]
    out_shape = [jax.ShapeDtypeStruct((M, N), F32)]
    args = [a, w, *xs, mod4]
    n_side = 0
    if ada_side is not None:
        n_side, side_in, side_out, side_shape = _ada_side_specs(*ada_side, nb * ni, lambda j, i: j * ni + i)
        in_specs += side_in
        out_specs.append(side_out)
        out_shape.append(side_shape)
        args += list(ada_side[:3])
    res = pl.pallas_call(
        functools.partial(_mm_resid_kernel, n_x=len(xs), n_a=xs[0].shape[0] // bm, n_side=n_side),
        grid=(nb, ni),
        in_specs=in_specs,
        out_specs=out_specs,
        out_shape=out_shape,
        compiler_params=_params(2, vmem_mb),
        name=name,
    )(*args)
    return res if n_side else res[0]


def _swiglu_kernel(a_ref, wg_ref, wu_ref, wd_ref, o_ref, wd_o, *, n_cast):
    wg = wg_ref[0].astype(BF16)
    wu = wu_ref[0].astype(BF16)
    for rows in _slabs(a_ref.shape[0]):
        a = a_ref[rows, :]
        g = _dot(a, wg)
        u = _dot(a, wu)
        o_ref[rows, :] = (g * jax.nn.sigmoid(g) * u).astype(o_ref.dtype)

    @pl.when(pl.program_id(0) * pl.num_programs(1) + pl.program_id(1) < n_cast)
    def _():
        wd_o[...] = wd_ref[0].astype(wd_o.dtype)


def _swiglu_up(a, wg, wu, wd, layer, bm, bn):
    M, K = a.shape
    N = wg.shape[2]
    KD, ND = wd.shape[1:]
    bm = min(bm, M)
    nj = N // bn
    n_steps = (M // bm) * nj
    n_cast = max(n for n in range(1, n_steps + 1) if KD % n == 0 and (KD // n) % 16 == 0)
    rb = KD // n_cast
    cast_blk = lambda i, j: jnp.minimum(i * nj + j, n_cast - 1)
    return pl.pallas_call(
        functools.partial(_swiglu_kernel, n_cast=n_cast),
        grid=(M // bm, nj),
        in_specs=[pl.BlockSpec((bm, K), lambda i, j: (i, 0)),
                  pl.BlockSpec((1, K, bn), lambda i, j: (layer, 0, j)),
                  pl.BlockSpec((1, K, bn), lambda i, j: (layer, 0, j)),
                  pl.BlockSpec((1, rb, ND), lambda i, j: (layer, cast_blk(i, j), 0))],
        out_specs=[pl.BlockSpec((bm, bn), lambda i, j: (i, j)),
                   pl.BlockSpec((rb, ND), lambda i, j: (cast_blk(i, j), 0))],
        out_shape=[jax.ShapeDtypeStruct((M, N), BF16),
                   jax.ShapeDtypeStruct((KD, ND), BF16)],
        compiler_params=_params(2, 60),
        name="swiglu_up",
    )(a, wg, wu, wd)


def _merge_kernel(af_ref, ap_ref, aoc_ref, aol_ref, wf_ref, wp_ref, wo_ref, g0_ref, g1_ref, g2_ref, o_ref, *, n_a):
    def run(ao_ref):
        for rows in _slabs(af_ref.shape[0]):
            y = g0_ref[rows, :].astype(F32) * _dot(af_ref[rows, :], wf_ref[0])
            y = y + g1_ref[rows, :].astype(F32) * _dot(ap_ref[rows, :], wp_ref[0])
            y = y + g2_ref[rows, :].astype(F32) * _dot(ao_ref[rows, :], wo_ref[0])
            o_ref[rows, :] = y.astype(o_ref.dtype)

    pl.when(pl.program_id(1) < n_a)(lambda: run(aoc_ref))
    pl.when(pl.program_id(1) >= n_a)(lambda: run(aol_ref))


def _merge(af, ap, aos, wf, wp, wo, gates, layer, bm, bn):
    M = af.shape[0]
    N = wf.shape[2]
    bm = min(bm, aos[0].shape[0], aos[1].shape[0])
    nb = N // bn
    a_spec = lambda a: pl.BlockSpec((bm, a.shape[1]), lambda j, i: (i, 0))
    w_spec = lambda w: pl.BlockSpec((1, w.shape[1], bn), lambda j, i: (layer, 0, j))
    g_spec = lambda k: pl.BlockSpec((bm, bn), lambda j, i: (i, k * nb + j))
    return pl.pallas_call(
        functools.partial(_merge_kernel, n_a=aos[0].shape[0] // bm),
        grid=(nb, M // bm),
        in_specs=[a_spec(af), a_spec(ap)] + _row_specs(aos, bm, aos[0].shape[1], 1) + [
                  w_spec(wf), w_spec(wp), w_spec(wo), g_spec(0), g_spec(1), g_spec(2)],
        out_specs=pl.BlockSpec((bm, bn), lambda j, i: (i, j)),
        out_shape=jax.ShapeDtypeStruct((M, N), BF16),
        compiler_params=_params(2, 60),
        name="merge",
    )(af, ap, *aos, wf, wp, wo, gates, gates, gates)


def _dft_group(f_ref, ch_ref, pos_ref, o_ref, g, s, fg):
    tch = f_ref.shape[0]
    scale = 1.0 / float(np.sqrt(s * fg))
    x = f_ref[:, g * fg:(g + 1) * fg].astype(BF16)
    xcs = _dot(x, ch_ref[...]).astype(BF16)
    for q in range(tch // s):
        rows = slice(q * s, (q + 1) * s)
        stacked = jnp.concatenate([xcs[rows, :fg], xcs[rows, fg:]], axis=0)
        y = _dot(pos_ref[...], stacked) * scale
        o_ref[rows, g * fg:(g + 1) * fg] = y.astype(o_ref.dtype)


def _pool_group(p_ref, ps_ref, o_ref, gi, s, pg):
    tch = p_ref.shape[0]
    w = POOL_WINDOWS[gi]
    half = w // 2
    assert w == 2 * half and half & (half - 1) == 0
    pos = jax.lax.broadcasted_iota(jnp.int32, (tch, pg), 0) & (s - 1)
    cols = slice(gi * pg, (gi + 1) * pg)
    x = p_ref[:, cols]
    before = jnp.where(pos >= 1, pltpu.roll(x, 1, 0), 0.0)
    after = x
    m = 1
    while m < half:
        before = before + jnp.where(pos >= m, pltpu.roll(before, m, 0), 0.0)
        after = after + jnp.where(pos + m < s, pltpu.roll(after, tch - m, 0), 0.0)
        m *= 2
    cnt = (jnp.minimum(pos + half, s) - jnp.maximum(pos - half, 0)).astype(F32)
    o_ref[:, cols] = (((before + after) / cnt - x) * ps_ref[:, cols]).astype(o_ref.dtype)


def _dft_kernel(f_ref, ch_ref, pc_ref, pl_ref, o_ref, *, n_ctx_chunks, s_ctx, fg):
    i = pl.program_id(0)

    @pl.when(i < n_ctx_chunks)
    def _():
        for g in range(N_FOURIER_GROUPS):
            _dft_group(f_ref, ch_ref, pc_ref, o_ref, g, s_ctx, fg)

    @pl.when(i >= n_ctx_chunks)
    def _():
        for g in range(N_FOURIER_GROUPS):
            _dft_group(f_ref, ch_ref, pl_ref, o_ref, g, f_ref.shape[0], fg)


def _pool_kernel(p_ref, ps_ref, o_ref, *, n_ctx_chunks, s_ctx, pg):
    i = pl.program_id(0)

    @pl.when(i < n_ctx_chunks)
    def _():
        for gi in range(len(POOL_WINDOWS)):
            _pool_group(p_ref, ps_ref, o_ref, gi, s_ctx, pg)

    @pl.when(i >= n_ctx_chunks)
    def _():
        for gi in range(len(POOL_WINDOWS)):
            _pool_group(p_ref, ps_ref, o_ref, gi, p_ref.shape[0], pg)


def _dft_tables(s_ctx, s_lat, fg):
    def cs(n):
        k = np.arange(n, dtype=np.int64)
        ang = 2.0 * np.pi * ((k[:, None] * k[None, :]) % n).astype(np.float64) / n
        return np.cos(ang), np.sin(ang)
    cc, sc = cs(fg)
    chan = np.concatenate([cc, sc], axis=1)
    pos = []
    for s in (s_ctx, s_lat):
        c, sn = cs(s)
        pos.append(np.concatenate([c, -sn], axis=1))
    return (jnp.asarray(chan, BF16), jnp.asarray(pos[0], BF16), jnp.asarray(pos[1], BF16))


def _dft(z, dims, df):
    M = z.shape[0]
    tch, s_ctx = dims["tch"], dims["s_ctx"]
    fg = df // N_FOURIER_GROUPS
    chan, pos_c, pos_l = _dft_tables(s_ctx, tch, fg)
    full = lambda a: pl.BlockSpec(a.shape, lambda i: (0, 0))
    return pl.pallas_call(
        functools.partial(_dft_kernel, n_ctx_chunks=dims["m_ctx"] // tch, s_ctx=s_ctx, fg=fg),
        grid=(M // tch,),
        in_specs=[pl.BlockSpec((tch, df), lambda i: (i, 0)), full(chan), full(pos_c), full(pos_l)],
        out_specs=pl.BlockSpec((tch, df), lambda i: (i, 0)),
        out_shape=jax.ShapeDtypeStruct((M, df), BF16),
        compiler_params=_params(1, 48),
        name="dft_mix",
    )(z, chan, pos_c, pos_l)


def _pool(z, pool_scale_l, dims, dp, col_blk):
    M = z.shape[0]
    tch = dims["tch"]
    return pl.pallas_call(
        functools.partial(_pool_kernel, n_ctx_chunks=dims["m_ctx"] // tch, s_ctx=dims["s_ctx"],
                          pg=dp // len(POOL_WINDOWS)),
        grid=(M // tch,),
        in_specs=[pl.BlockSpec((tch, dp), lambda i: (i, col_blk)),
                  pl.BlockSpec((1, dp), lambda i: (0, 0))],
        out_specs=pl.BlockSpec((tch, dp), lambda i: (i, 0)),
        out_shape=jax.ShapeDtypeStruct((M, dp), BF16),
        compiler_params=_params(1, 32),
        name="pool",
    )(z, pool_scale_l.reshape(1, dp))


def _rope_tables(n_tokens):
    rows = n_tokens // GRID_W
    row = np.repeat(np.arange(rows, dtype=np.float64), GRID_W)
    col = np.tile(np.arange(GRID_W, dtype=np.float64), rows)
    inv = ROPE_BASE ** (-np.arange(ROPE_DIM // 4, dtype=np.float64) * (4.0 / ROPE_DIM))
    ar = row[:, None] * inv[None, :]
    ac = col[:, None] * inv[None, :]
    ang = np.concatenate([ar, ar, ac, ac], axis=1)
    sign = np.tile(np.concatenate([-np.ones(ROPE_DIM // 4), np.ones(ROPE_DIM // 4)]), 2)
    cos = np.concatenate([np.cos(ang), np.ones_like(ang)], axis=1)
    sin = np.concatenate([np.sin(ang) * sign[None, :], np.zeros_like(ang)], axis=1)
    return jnp.asarray(cos, F32), jnp.asarray(sin, F32)


def _rope(x, cos, sin):
    q = ROPE_DIM // 4
    lane = jax.lax.broadcasted_iota(jnp.int32, x.shape, 1)
    first = (lane & (2 * q - 1)) < q
    swapped = jnp.where(first, pltpu.roll(x, LANE - q, 1), pltpu.roll(x, q, 1))
    return x * cos + swapped * sin


def _q_kernel(cq_ref, g_ref, w_ref, cos_ref, sin_ref, o_ref, wp_ref, *, n_ctx_tiles):
    i = pl.program_id(0)

    @pl.when(i == 0)
    def _():
        hd = NOPE_DIM + ROPE_DIM
        for h in range(N_HEADS):
            wp_ref[:, h * HEAD_PAD:h * HEAD_PAD + hd] = w_ref[0, :, h * hd:(h + 1) * hd].astype(BF16)
            wp_ref[:, h * HEAD_PAD + hd:(h + 1) * HEAD_PAD] = jnp.zeros((wp_ref.shape[0], HEAD_PAD - hd), BF16)

    def q_rows(rows):
        x = cq_ref[rows, :]
        r = jax.lax.rsqrt(jnp.mean(x * x, axis=-1, keepdims=True) + EPS)
        return _dot((x * r * g_ref[...]).astype(BF16), wp_ref[...]) * Q_SCALE

    @pl.when(i < n_ctx_tiles)
    def _():
        for rows in _slabs(cq_ref.shape[0]):
            o_ref[rows, :] = q_rows(rows).astype(o_ref.dtype)

    @pl.when(i >= n_ctx_tiles)
    def _():
        for rows in _slabs(cq_ref.shape[0]):
            q = q_rows(rows)
            cos, sin = cos_ref[rows, :], sin_ref[rows, :]
            for h in range(N_HEADS):
                lo = h * HEAD_PAD
                o_ref[rows, lo:lo + NOPE_DIM] = q[:, lo:lo + NOPE_DIM].astype(o_ref.dtype)
                pe = q[:, lo + NOPE_DIM:lo + HEAD_PAD]
                o_ref[rows, lo + NOPE_DIM:lo + HEAD_PAD] = _rope(pe, cos, sin).astype(o_ref.dtype)


def _q_proj(z, g_q_l, w_q_b, layer, cos, sin, dims, ql, col_blk):
    M = z.shape[0]
    N = N_HEADS * HEAD_PAD
    bm = min(512, dims["tch"])
    n_ctx_tiles = dims["m_ctx"] // bm
    per_seq = dims["tch"] // bm
    pos_blk = lambda i: (jnp.maximum(i - n_ctx_tiles, 0) % per_seq, 0)
    return pl.pallas_call(
        functools.partial(_q_kernel, n_ctx_tiles=n_ctx_tiles),
        grid=(M // bm,),
        in_specs=[pl.BlockSpec((bm, ql), lambda i: (i, col_blk)),
                  pl.BlockSpec((1, ql), lambda i: (0, 0)),
                  pl.BlockSpec((1,) + w_q_b.shape[1:], lambda i: (layer, 0, 0), pipeline_mode=pl.Buffered(1)),
                  pl.BlockSpec((bm, LANE), pos_blk),
                  pl.BlockSpec((bm, LANE), pos_blk)],
        out_specs=pl.BlockSpec((bm, N), lambda i: (i, 0)),
        out_shape=jax.ShapeDtypeStruct((M, N), BF16),
        scratch_shapes=[pltpu.VMEM((ql, N), BF16)],
        compiler_params=_params(1, 56),
        name="q_proj",
    )(z, g_q_l.reshape(1, ql), w_q_b, cos, sin)


def _kv_kernel(ckv_ref, g_ref, w_ref, kpe_ref, cos_ref, sin_ref, ckv_o, kv_o, kpe_o, *, n_ctx_tiles):
    i = pl.program_id(0)
    for rows in _slabs(ckv_ref.shape[0]):
        x = ckv_ref[rows, :]
        r = jax.lax.rsqrt(jnp.mean(x * x, axis=-1, keepdims=True) + EPS)
        c = x * r * g_ref[...]
        ckv_o[rows, :] = c
        kv_o[rows, :] = _dot(c.astype(BF16), w_ref[...]).astype(kv_o.dtype)
    kpe = kpe_ref[...]
    kpe = jnp.where(jax.lax.broadcasted_iota(jnp.int32, kpe.shape, 1) < ROPE_DIM, kpe, 0.0)

    @pl.when(i < n_ctx_tiles)
    def _():
        kpe_o[...] = kpe.astype(kpe_o.dtype)

    @pl.when(i >= n_ctx_tiles)
    def _():
        kpe_o[...] = _rope(kpe, cos_ref[...], sin_ref[...]).astype(kpe_o.dtype)


def _kv_proj(z, kpe_raw, g_kv_l, w_kv, cos, sin, dims, kvl, ckv_blk):
    M = z.shape[0]
    N = w_kv.shape[1]
    bm = min(512, dims["tch"])
    n_ctx_tiles = dims["m_ctx"] // bm
    per_seq = dims["tch"] // bm
    pos_blk = lambda i: (jnp.maximum(i - n_ctx_tiles, 0) % per_seq, 0)
    return pl.pallas_call(
        functools.partial(_kv_kernel, n_ctx_tiles=n_ctx_tiles),
        grid=(M // bm,),
        in_specs=[pl.BlockSpec((bm, kvl), lambda i: (i, ckv_blk)),
                  pl.BlockSpec((1, kvl), lambda i: (0, 0)),
                  pl.BlockSpec((kvl, N), lambda i: (0, 0)),
                  pl.BlockSpec((bm, LANE), lambda i: (i, 0)),
                  pl.BlockSpec((bm, LANE), pos_blk),
                  pl.BlockSpec((bm, LANE), pos_blk)],
        out_specs=[pl.BlockSpec((bm, kvl), lambda i: (i, 0)),
                   pl.BlockSpec((bm, N), lambda i: (i, 0)),
                   pl.BlockSpec((bm, LANE), lambda i: (i, 0))],
        out_shape=[jax.ShapeDtypeStruct((M, kvl), F32),
                   jax.ShapeDtypeStruct((M, N), BF16),
                   jax.ShapeDtypeStruct((M, LANE), BF16)],
        compiler_params=_params(1, 48),
        name="kv_proj",
    )(z, g_kv_l.reshape(1, kvl), w_kv, kpe_raw, cos, sin)


def _kv_cache_kernel(c_ref, w_ref, o_ref):
    o_ref[...] = _dot(c_ref[0, 0].astype(BF16), w_ref[...]).astype(o_ref.dtype)


def _kv_cache_proj(cache_ckv, w_kv, layer):
    DB, _, P, kvl = cache_ckv.shape
    N = w_kv.shape[1]
    return pl.pallas_call(
        _kv_cache_kernel,
        grid=(DB,),
        in_specs=[pl.BlockSpec((1, 1, P, kvl), lambda b: (b, layer, 0, 0)),
                  pl.BlockSpec((kvl, N), lambda b: (0, 0))],
        out_specs=pl.BlockSpec((P, N), lambda b: (b, 0)),
        out_shape=jax.ShapeDtypeStruct((DB * P, N), BF16),
        compiler_params=_params(1, 32),
        name="kv_cache_proj",
    )(cache_ckv, w_kv)


def _attend(qh, keys, values):
    ss = [_dot_nt(qh, k) for k in keys]
    m = functools.reduce(jnp.maximum, [jnp.max(s, axis=-1, keepdims=True) for s in ss])
    acc = None
    for s, v in zip(ss, values):
        p = jnp.exp2(s - m).astype(BF16)
        part = _dot(p, jnp.concatenate([v, jnp.ones_like(v)], axis=1))
        acc = part if acc is None else acc + part
    return acc[:, :V_DIM] * (1.0 / acc[:, V_DIM:])


def _attn_ctx_kernel(q_ref, kv_ref, kpe_ref, o_ref):
    kpe = kpe_ref[...]
    for h in range(N_HEADS):
        lo = h * HEAD_PAD
        k = jnp.concatenate([kv_ref[:, lo:lo + NOPE_DIM], kpe], axis=1)
        o = _attend(q_ref[:, lo:lo + HEAD_PAD], [k], [kv_ref[:, lo + NOPE_DIM:lo + HEAD_PAD]])
        o_ref[:, h * V_DIM:(h + 1) * V_DIM] = o.astype(o_ref.dtype)


def _attn_ctx(q, kv, kpe, dims):
    M = q.shape[0]
    s = dims["s_ctx"]
    n = dims["m_ctx"] // s
    dv = N_HEADS * V_DIM
    return pl.pallas_call(
        _attn_ctx_kernel,
        grid=(n,),
        in_specs=[pl.BlockSpec((s, q.shape[1]), lambda b: (b, 0)),
                  pl.BlockSpec((s, kv.shape[1]), lambda b: (b, 0)),
                  pl.BlockSpec((s, LANE), lambda b: (b, 0))],
        out_specs=pl.BlockSpec((s, dv), lambda b: (b, 0)),
        out_shape=jax.ShapeDtypeStruct((dims["m_ctx"], dv), BF16),
        compiler_params=_params(1, 32),
        name="attn_ctx",
    )(q, kv, kpe)


def _attn_lat_kernel(q_ref, kv_ref, kpe_ref, kvc_ref, kpec_ref, o_ref):
    kpe = kpe_ref[...]
    kpec = kpec_ref[0, 0].astype(BF16)
    kpec = jnp.concatenate([kpec, jnp.zeros_like(kpec)], axis=1)
    for h in range(N_HEADS):
        lo = h * HEAD_PAD
        qh = q_ref[:, lo:lo + HEAD_PAD]
        k_own = jnp.concatenate([kv_ref[:, lo:lo + NOPE_DIM], kpe], axis=1)
        k_c = jnp.concatenate([kvc_ref[:, lo:lo + NOPE_DIM], kpec], axis=1)
        o = _attend(qh, [k_c, k_own], [kvc_ref[:, lo + NOPE_DIM:lo + HEAD_PAD], kv_ref[:, lo + NOPE_DIM:lo + HEAD_PAD]])
        o_ref[:, h * V_DIM:(h + 1) * V_DIM] = o.astype(o_ref.dtype)


def _attn_lat(q, kv, kpe, kv_cache, cache_krope, layer, dims):
    tch, m_ctx = dims["tch"], dims["m_ctx"]
    DB, _, P, rd = cache_krope.shape
    bq = min(256, tch)
    nq = tch // bq
    dv = N_HEADS * V_DIM
    q_blk = lambda b, t: (m_ctx // bq + b * nq + t, 0)
    own_blk = lambda b, t: (m_ctx // tch + b, 0)
    return pl.pallas_call(
        _attn_lat_kernel,
        grid=(DB, nq),
        in_specs=[pl.BlockSpec((bq, q.shape[1]), q_blk),
                  pl.BlockSpec((tch, kv.shape[1]), own_blk),
                  pl.BlockSpec((tch, LANE), own_blk),
                  pl.BlockSpec((P, kv_cache.shape[1]), lambda b, t: (b, 0)),
                  pl.BlockSpec((1, 1, P, rd), lambda b, t: (b, layer, 0, 0))],
        out_specs=pl.BlockSpec((bq, dv), lambda b, t: (b * nq + t, 0)),
        out_shape=jax.ShapeDtypeStruct((DB * tch, dv), BF16),
        compiler_params=_params(2, 48),
        name="attn_lat",
    )(q, kv, kpe, kv_cache, cache_krope)


def kernel(x_prompt, x_sample, cache_ckv, cache_krope, c, c_ctx, w_ada, b_ada, g_norm1, g_norm2, w_in, w_fourier,
           pool_scale, w_pool, g_q, w_q_b, g_kv, w_kv_b, w_o_mla, w_out, w_gate, w_up, w_down, g_final):
    B, S, D = x_prompt.shape
    DB, DS, _ = x_sample.shape
    L = w_ada.shape[0]
    DF = w_fourier.shape[1]
    DP = pool_scale.shape[1]
    QL = g_q.shape[1]
    KVL = g_kv.shape[1]
    m_ctx, m_lat = B * S, DB * DS
    M = m_ctx + m_lat
    assert DS % S == 0 and m_ctx % DS == 0 and S & (S - 1) == 0 and DS & (DS - 1) == 0
    assert 1 + DB <= MOD_ROWS and DS % GRID_W == 0
    dims = dict(m_ctx=m_ctx, tch=DS, s_ctx=S)

    off_p, off_q, off_kv, off_pe = DF, DF + DP, DF + DP + QL, DF + DP + QL + KVL
    assert off_p % DP == 0 and off_q % QL == 0 and off_kv % KVL == 0 and off_pe % IN_BN == 0

    xs = (x_prompt.reshape(m_ctx, D), x_sample.reshape(m_lat, D))
    cond = jnp.zeros((MOD_ROWS, D), F32).at[0].set(c_ctx).at[1:1 + DB].set(c)
    b_ada3 = b_ada.reshape(L, 1, 6 * D)
    mod4 = _ada_mod(cond, w_ada, b_ada3, 0).reshape(1, MOD_ROWS, 1, 6 * D)
    cos, sin = _rope_tables(DS)
    w_in_t = jnp.swapaxes(w_in, 1, 2)
    wf_b, wp_b, wo_b = w_fourier.astype(BF16), w_pool.reshape(L, DP, D).astype(BF16), w_o_mla.astype(BF16)
    w_out_b = w_out.astype(BF16)

    ckv_layers, kpe_layers = [], []
    for l in range(L):
        w_kv = w_kv_b[l].astype(BF16)

        h = _norm_mod(xs, g_norm1[l:l + 1], mod4, 0, dims)
        z, kpe_raw, gates = _in_proj(h, w_in_t, l, off_pe, off_pe + ROPE_DIM, 2048)

        mixed = _dft(z, dims, DF)
        pooled = _pool(z, pool_scale[l], dims, DP, off_p // DP)
        q = _q_proj(z, g_q[l], w_q_b, l, cos, sin, dims, QL, off_q // QL)
        c_kv, kv, kpe = _kv_proj(z, kpe_raw, g_kv[l], w_kv, cos, sin, dims, KVL, off_kv // KVL)
        kv_cache = _kv_cache_proj(cache_ckv, w_kv, l)
        o_ctx = _attn_ctx(q, kv, kpe, dims)
        o_lat = _attn_lat(q, kv, kpe, kv_cache, cache_krope, l, dims)

        merged = _merge(mixed, pooled, (o_ctx, o_lat), wf_b, wp_b, wo_b, gates, l, 1024, 1024)
        x = _mm_resid(merged, w_out_b, xs, mod4, l, 2, 1024, 1024, dims, 60, "out_proj")

        h = _norm_mod((x,), g_norm2[l:l + 1], mod4, 3, dims)
        act, w_down_b = _swiglu_up(h, w_gate, w_up, w_down, l, 2048, 256)
        if l + 1 < L:
            x, mod_next = _mm_resid(act, w_down_b[None], (x,), mod4, 0, 5, 512, 512, dims, 60, "down_proj",
                                    ada_side=(cond, w_ada, b_ada3, l + 1))
            mod4 = mod_next.reshape(1, MOD_ROWS, 1, 6 * D)
        else:
            x = _mm_resid(act, w_down_b[None], (x,), mod4, 0, 5, 512, 512, dims, 56, "down_proj")
        xs = (x,)

        ckv_layers.append(c_kv[:m_ctx].reshape(B, S, KVL))
        kpe_layers.append(kpe_raw[:m_ctx, :ROPE_DIM].reshape(B, S, ROPE_DIM))

    y_prompt = _final_norm(x, g_final, 0, m_ctx).reshape(B, S, D)
    y_sample = _final_norm(x, g_final, m_ctx, m_lat).reshape(DB, DS, D)
    return (y_prompt, y_sample, jnp.stack(ckv_layers, axis=1), jnp.stack(kpe_layers, axis=1))
```

```python
import functools

import numpy as np
import jax
import jax.numpy as jnp
from jax.experimental import pallas as pl
from jax.experimental.pallas import tpu as pltpu

N_HEADS = 16
NOPE_DIM = 128
ROPE_DIM = 64
V_DIM = 128
HEAD_PAD = 256
GRID_W = 64
EPS = 1e-6
ROPE_BASE = 10000.0
SM_SCALE = (NOPE_DIM + ROPE_DIM) ** -0.5
Q_SCALE = SM_SCALE * float(np.log2(np.e))
POOL_WINDOWS = (2, 4, 8, 16)
N_FOURIER_GROUPS = 4
MOD_ROWS = 8
LANE = 128

BF16 = jnp.bfloat16
F32 = jnp.float32


def _params(n_grid, vmem_mb):
    return pltpu.CompilerParams(dimension_semantics=("arbitrary",) * n_grid,
                                vmem_limit_bytes=vmem_mb * 2 ** 20)


def _dot(a, b):
    return jnp.dot(a, b, preferred_element_type=F32)


def _dot_nt(a, b):
    return jax.lax.dot_general(a, b, (((1,), (1,)), ((), ())), preferred_element_type=F32)


SLAB_ROWS = 256


def _slabs(n_rows, slab_rows=SLAB_ROWS):
    r = int(np.gcd(slab_rows, n_rows))
    return [slice(s * r, (s + 1) * r) for s in range(n_rows // r)]


def _mod_row(i, bm, m_ctx, tch):
    start = i * bm
    return jnp.where(start < m_ctx, 0, 1 + (start - m_ctx) // tch)


def _ada_kernel(c_ref, w_ref, b_ref, o_ref):
    c = c_ref[...]
    a = (c * jax.nn.sigmoid(c)).astype(BF16)
    o_ref[0] = _dot(a, w_ref[0].astype(BF16)) + b_ref[0]


def _ada_mod(cond, w_ada, b_ada3, layer):
    _, D, N = w_ada.shape
    bn = 512
    return pl.pallas_call(
        _ada_kernel,
        grid=(N // bn,),
        in_specs=[pl.BlockSpec((MOD_ROWS, D), lambda j: (0, 0)),
                  pl.BlockSpec((1, D, bn), lambda j: (layer, 0, j)),
                  pl.BlockSpec((1, 1, bn), lambda j: (layer, 0, j))],
        out_specs=pl.BlockSpec((1, MOD_ROWS, bn), lambda j: (0, 0, j)),
        out_shape=jax.ShapeDtypeStruct((1, MOD_ROWS, N), F32),
        compiler_params=_params(1, 40),
        name="ada_mod",
    )(cond, w_ada, b_ada3)


def _ada_side_specs(cond, w_ada, b_ada3, layer, n_steps, step_of):
    _, D, N = w_ada.shape
    bn = min(t for t in range(2 * LANE, N + 1, 2 * LANE) if N % t == 0 and N // t <= n_steps)
    n_side = N // bn
    blk = lambda *g: jnp.minimum(step_of(*g), n_side - 1)
    in_specs = [pl.BlockSpec((MOD_ROWS, D), lambda *g: (0, 0)),
                pl.BlockSpec((1, D, bn), lambda *g: (layer, 0, blk(*g))),
                pl.BlockSpec((1, 1, bn), lambda *g: (layer, 0, blk(*g)))]
    out_spec = pl.BlockSpec((1, MOD_ROWS, bn), lambda *g: (0, 0, blk(*g)))
    out_shape = jax.ShapeDtypeStruct((1, MOD_ROWS, N), F32)
    return n_side, in_specs, out_spec, out_shape


def _row_specs(xs, bm, bn, n_grid):
    idx = (lambda *g: (g[-1], g[0])) if n_grid == 2 else (lambda *g: (g[-1], 0))
    if len(xs) == 1:
        return [pl.BlockSpec((bm, bn), idx)]
    n_a = xs[0].shape[0] // bm
    last_b = xs[1].shape[0] // bm - 1
    spec_a = pl.BlockSpec((bm, bn), lambda *g: (jnp.minimum(idx(*g)[0], n_a - 1), idx(*g)[1]))
    spec_b = pl.BlockSpec((bm, bn), lambda *g: (jnp.clip(idx(*g)[0] - n_a, 0, last_b), idx(*g)[1]))
    return [spec_a, spec_b]


def _norm_mod_kernel(*refs, n_x, n_a):
    x_refs, (g_ref, sh_ref, sc_ref, o_ref) = refs[:n_x], refs[n_x:]
    x = x_refs[0][...] if n_x == 1 else jnp.where(pl.program_id(0) < n_a, x_refs[0][...], x_refs[1][...])
    r = jax.lax.rsqrt(jnp.mean(x * x, axis=-1, keepdims=True) + EPS)
    y = x * r * g_ref[...]
    o_ref[...] = (y * (1.0 + sc_ref[0, 0]) + sh_ref[0, 0]).astype(o_ref.dtype)


def _norm_mod(xs, g, mod4, shift_blk, dims):
    M = sum(x.shape[0] for x in xs)
    D = xs[0].shape[1]
    bm = min(512, dims["tch"])
    row = functools.partial(_mod_row, bm=bm, m_ctx=dims["m_ctx"], tch=dims["tch"])
    return pl.pallas_call(
        functools.partial(_norm_mod_kernel, n_x=len(xs), n_a=xs[0].shape[0] // bm),
        grid=(M // bm,),
        in_specs=_row_specs(xs, bm, D, 1) + [
            pl.BlockSpec((1, D), lambda i: (0, 0)),
            pl.BlockSpec((1, 1, 1, D), lambda i: (0, row(i), 0, shift_blk)),
            pl.BlockSpec((1, 1, 1, D), lambda i: (0, row(i), 0, shift_blk + 1))],
        out_specs=pl.BlockSpec((bm, D), lambda i: (i, 0)),
        out_shape=jax.ShapeDtypeStruct((M, D), BF16),
        compiler_params=_params(1, 56),
        name="norm_mod",
    )(*xs, g, mod4, mod4)


def _final_norm_kernel(x_ref, g_ref, o_ref):
    x = x_ref[...]
    r = jax.lax.rsqrt(jnp.mean(x * x, axis=-1, keepdims=True) + EPS)
    o_ref[...] = x * r * g_ref[...]


def _final_norm(x, g, row0, rows):
    D = x.shape[1]
    bm = int(np.gcd(512, np.gcd(rows, row0 if row0 else rows)))
    blk0 = row0 // bm
    return pl.pallas_call(
        _final_norm_kernel,
        grid=(rows // bm,),
        in_specs=[pl.BlockSpec((bm, D), lambda i: (blk0 + i, 0)),
                  pl.BlockSpec((1, D), lambda i: (0, 0))],
        out_specs=pl.BlockSpec((bm, D), lambda i: (i, 0)),
        out_shape=jax.ShapeDtypeStruct((rows, D), F32),
        compiler_params=_params(1, 48),
        name="final_norm",
    )(x, g.reshape(1, D))


IN_BN = 512


def _in_proj_kernel(h_ref, w_ref, wk_ref, z_ref, kpe_ref, g_ref, *, n_small_blk):
    j = pl.program_id(1)

    @pl.when(j < n_small_blk)
    def _():
        w = w_ref[0].astype(BF16)
        for rows in _slabs(h_ref.shape[0]):
            z_ref[rows, :] = _dot_nt(h_ref[rows, :], w)

    @pl.when(j == n_small_blk)
    def _():
        kpe_ref[...] = _dot_nt(h_ref[...], wk_ref[0].astype(BF16))

    @pl.when(j >= n_small_blk)
    def _():
        w = w_ref[0].astype(BF16)
        for rows in _slabs(h_ref.shape[0]):
            g_ref[rows, :] = jax.nn.sigmoid(_dot_nt(h_ref[rows, :], w)).astype(g_ref.dtype)


def _in_proj(h, w_t, layer, off_pe, n_small, bm):
    M, K = h.shape
    N = w_t.shape[1]
    bm = min(bm, M)
    n_small_blk = off_pe // IN_BN
    n_gate_blk = (N - n_small) // IN_BN
    assert off_pe % IN_BN == 0 and (N - n_small) % IN_BN == 0 and n_small % 8 == 0 and off_pe % LANE == 0

    def w_row(j):
        row = jnp.where(j < n_small_blk, j * IN_BN, n_small + (j - n_small_blk) * IN_BN)
        return pl.multiple_of(row, 8)

    return pl.pallas_call(
        functools.partial(_in_proj_kernel, n_small_blk=n_small_blk),
        grid=(M // bm, n_small_blk + n_gate_blk),
        in_specs=[pl.BlockSpec((bm, K), lambda i, j: (i, 0), pipeline_mode=pl.Buffered(1)),
                  pl.BlockSpec((pl.Element(1), pl.Element(IN_BN), pl.Element(K)), lambda i, j: (layer, w_row(j), 0)),
                  pl.BlockSpec((1, LANE, K), lambda i, j: (layer, off_pe // LANE, 0), pipeline_mode=pl.Buffered(1))],
        out_specs=[pl.BlockSpec((bm, IN_BN), lambda i, j: (i, jnp.minimum(j, n_small_blk - 1))),
                   pl.BlockSpec((bm, LANE), lambda i, j: (i, 0)),
                   pl.BlockSpec((bm, IN_BN), lambda i, j: (i, jnp.maximum(j - n_small_blk, 0)))],
        out_shape=[jax.ShapeDtypeStruct((M, off_pe), F32),
                   jax.ShapeDtypeStruct((M, LANE), F32),
                   jax.ShapeDtypeStruct((M, n_gate_blk * IN_BN), BF16)],
        compiler_params=_params(2, 60),
        name="in_proj",
    )(h, w_t, w_t)


def _mm_resid_kernel(a_ref, w_ref, *refs, n_x, n_a, n_side):
    x_refs, rest = refs[:n_x], refs[n_x:]
    gate_ref, o_ref = rest[0], rest[-2 if n_side else -1]
    first = pl.program_id(1) < n_a
    for rows in _slabs(a_ref.shape[0]):
        x = x_refs[0][rows, :] if n_x == 1 else jnp.where(first, x_refs[0][rows, :], x_refs[1][rows, :])
        o_ref[rows, :] = x + gate_ref[0, 0] * _dot(a_ref[rows, :], w_ref[0])

    if n_side:
        c_ref, wa_ref, ba_ref, mod_o = rest[1], rest[2], rest[3], rest[-1]

        @pl.when(pl.program_id(0) * pl.num_programs(1) + pl.program_id(1) < n_side)
        def _():
            _ada_kernel(c_ref, wa_ref, ba_ref, mod_o)


def _mm_resid(a, w, xs, mod4, layer, gate_blk, bm, bn, dims, vmem_mb, name, ada_side=None):
    M, K = a.shape
    N = w.shape[2]
    bm = min(bm, dims["tch"])
    row = functools.partial(_mod_row, bm=bm, m_ctx=dims["m_ctx"], tch=dims["tch"])
    nb, ni = N // bn, M // bm
    in_specs = [pl.BlockSpec((bm, K), lambda j, i: (i, 0)),
                pl.BlockSpec((1, K, bn), lambda j, i: (layer, 0, j))] + _row_specs(xs, bm, bn, 2) + [
                pl.BlockSpec((1, 1, 1, bn), lambda j, i: (0, row(i), 0, gate_blk * nb + j))]
    out_specs = [pl.BlockSpec((bm, bn), lambda j, i: (i, j))]
    out_shape = [jax.ShapeDtypeStruct((M, N), F32)]
    args = [a, w, *xs, mod4]
    n_side = 0
    if ada_side is not None:
        n_side, side_in, side_out, side_shape = _ada_side_specs(*ada_side, nb * ni, lambda j, i: j * ni + i)
        in_specs += side_in
        out_specs.append(side_out)
        out_shape.append(side_shape)
        args += list(ada_side[:3])
    res = pl.pallas_call(
        functools.partial(_mm_resid_kernel, n_x=len(xs), n_a=xs[0].shape[0] // bm, n_side=n_side),
        grid=(nb, ni),
        in_specs=in_specs,
        out_specs=out_specs,
        out_shape=out_shape,
        compiler_params=_params(2, vmem_mb),
        name=name,
    )(*args)
    return res if n_side else res[0]


def _swiglu_kernel(a_ref, wg_ref, wu_ref, wd_ref, o_ref, wd_o, *, n_cast):
    wg = wg_ref[0].astype(BF16)
    wu = wu_ref[0].astype(BF16)
    for rows in _slabs(a_ref.shape[0]):
        a = a_ref[rows, :]
        g = _dot(a, wg)
        u = _dot(a, wu)
        o_ref[rows, :] = (g * jax.nn.sigmoid(g) * u).astype(o_ref.dtype)

    @pl.when(pl.program_id(0) * pl.num_programs(1) + pl.program_id(1) < n_cast)
    def _():
        wd_o[...] = wd_ref[0].astype(wd_o.dtype)


def _swiglu_up(a, wg, wu, wd, layer, bm, bn):
    M, K = a.shape
    N = wg.shape[2]
    KD, ND = wd.shape[1:]
    bm = min(bm, M)
    nj = N // bn
    n_steps = (M // bm) * nj
    n_cast = max(n for n in range(1, n_steps + 1) if KD % n == 0 and (KD // n) % 16 == 0)
    rb = KD // n_cast
    cast_blk = lambda i, j: jnp.minimum(i * nj + j, n_cast - 1)
    return pl.pallas_call(
        functools.partial(_swiglu_kernel, n_cast=n_cast),
        grid=(M // bm, nj),
        in_specs=[pl.BlockSpec((bm, K), lambda i, j: (i, 0)),
                  pl.BlockSpec((1, K, bn), lambda i, j: (layer, 0, j)),
                  pl.BlockSpec((1, K, bn), lambda i, j: (layer, 0, j)),
                  pl.BlockSpec((1, rb, ND), lambda i, j: (layer, cast_blk(i, j), 0))],
        out_specs=[pl.BlockSpec((bm, bn), lambda i, j: (i, j)),
                   pl.BlockSpec((rb, ND), lambda i, j: (cast_blk(i, j), 0))],
        out_shape=[jax.ShapeDtypeStruct((M, N), BF16),
                   jax.ShapeDtypeStruct((KD, ND), BF16)],
        compiler_params=_params(2, 60),
        name="swiglu_up",
    )(a, wg, wu, wd)


def _merge_kernel(af_ref, ap_ref, aoc_ref, aol_ref, wf_ref, wp_ref, wo_ref, g0_ref, g1_ref, g2_ref, o_ref, *, n_a):
    first = pl.program_id(1) < n_a
    for rows in _slabs(af_ref.shape[0]):
        ao = jnp.where(first, aoc_ref[rows, :], aol_ref[rows, :])
        y = g0_ref[rows, :].astype(F32) * _dot(af_ref[rows, :], wf_ref[0])
        y = y + g1_ref[rows, :].astype(F32) * _dot(ap_ref[rows, :], wp_ref[0])
        y = y + g2_ref[rows, :].astype(F32) * _dot(ao, wo_ref[0])
        o_ref[rows, :] = y.astype(o_ref.dtype)


def _merge(af, ap, aos, wf, wp, wo, gates, layer, bm, bn):
    M = af.shape[0]
    N = wf.shape[2]
    bm = min(bm, aos[0].shape[0], aos[1].shape[0])
    nb = N // bn
    a_spec = lambda a: pl.BlockSpec((bm, a.shape[1]), lambda j, i: (i, 0))
    w_spec = lambda w: pl.BlockSpec((1, w.shape[1], bn), lambda j, i: (layer, 0, j))
    g_spec = lambda k: pl.BlockSpec((bm, bn), lambda j, i: (i, k * nb + j))
    return pl.pallas_call(
        functools.partial(_merge_kernel, n_a=aos[0].shape[0] // bm),
        grid=(nb, M // bm),
        in_specs=[a_spec(af), a_spec(ap)] + _row_specs(aos, bm, aos[0].shape[1], 1) + [
                  w_spec(wf), w_spec(wp), w_spec(wo), g_spec(0), g_spec(1), g_spec(2)],
        out_specs=pl.BlockSpec((bm, bn), lambda j, i: (i, j)),
        out_shape=jax.ShapeDtypeStruct((M, N), BF16),
        compiler_params=_params(2, 60),
        name="merge",
    )(af, ap, *aos, wf, wp, wo, gates, gates, gates)


def _dft_group(f_ref, ch, pos, o_ref, g, s, fg):
    tch = f_ref.shape[0]
    scale = 1.0 / float(np.sqrt(s * fg))
    x = f_ref[:, g * fg:(g + 1) * fg].astype(BF16)
    xcs = _dot(x, ch).astype(BF16)
    for q in range(tch // s):
        rows = slice(q * s, (q + 1) * s)
        stacked = jnp.concatenate([xcs[rows, :fg], xcs[rows, fg:]], axis=0)
        y = _dot(pos, stacked) * scale
        o_ref[rows, g * fg:(g + 1) * fg] = y.astype(o_ref.dtype)


def _pool_group(p_ref, ps_ref, o_ref, gi, s, pg):
    tch = p_ref.shape[0]
    w = POOL_WINDOWS[gi]
    half = w // 2
    assert w == 2 * half and half & (half - 1) == 0
    pos = jax.lax.broadcasted_iota(jnp.int32, (tch, pg), 0) & (s - 1)
    cols = slice(gi * pg, (gi + 1) * pg)
    x = p_ref[:, cols]
    before = jnp.where(pos >= 1, pltpu.roll(x, 1, 0), 0.0)
    after = x
    m = 1
    while m < half:
        before = before + jnp.where(pos >= m, pltpu.roll(before, m, 0), 0.0)
        after = after + jnp.where(pos + m < s, pltpu.roll(after, tch - m, 0), 0.0)
        m *= 2
    cnt = (jnp.minimum(pos + half, s) - jnp.maximum(pos - half, 0)).astype(F32)
    o_ref[:, cols] = (((before + after) / cnt - x) * ps_ref[:, cols]).astype(o_ref.dtype)


def _dft_kernel(f_ref, ch_ref, pc_ref, pl_ref, o_ref, *, n_ctx_chunks, s_ctx, fg):
    i = pl.program_id(0)

    @pl.when(i < n_ctx_chunks)
    def _():
        ch, pos = ch_ref[...].astype(BF16), pc_ref[...].astype(BF16)
        for g in range(N_FOURIER_GROUPS):
            _dft_group(f_ref, ch, pos, o_ref, g, s_ctx, fg)

    @pl.when(i >= n_ctx_chunks)
    def _():
        ch, pos = ch_ref[...].astype(BF16), pl_ref[...].astype(BF16)
        for g in range(N_FOURIER_GROUPS):
            _dft_group(f_ref, ch, pos, o_ref, g, f_ref.shape[0], fg)


def _pool_kernel(p_ref, ps_ref, o_ref, *, n_ctx_chunks, s_ctx, pg):
    i = pl.program_id(0)

    @pl.when(i < n_ctx_chunks)
    def _():
        for gi in range(len(POOL_WINDOWS)):
            _pool_group(p_ref, ps_ref, o_ref, gi, s_ctx, pg)

    @pl.when(i >= n_ctx_chunks)
    def _():
        for gi in range(len(POOL_WINDOWS)):
            _pool_group(p_ref, ps_ref, o_ref, gi, p_ref.shape[0], pg)


def _dft_tables(s_ctx, s_lat, fg):
    def cs(n):
        k = np.arange(n, dtype=np.int64)
        ang = 2.0 * np.pi * ((k[:, None] * k[None, :]) % n).astype(np.float64) / n
        return np.cos(ang), np.sin(ang)
    cc, sc = cs(fg)
    chan = np.concatenate([cc, sc], axis=1)
    pos = []
    for s in (s_ctx, s_lat):
        c, sn = cs(s)
        pos.append(np.concatenate([c, -sn], axis=1))
    return (jnp.asarray(chan, F32), jnp.asarray(pos[0], F32), jnp.asarray(pos[1], F32))


def _dft(z, dims, df):
    M = z.shape[0]
    tch, s_ctx = dims["tch"], dims["s_ctx"]
    fg = df // N_FOURIER_GROUPS
    chan, pos_c, pos_l = _dft_tables(s_ctx, tch, fg)
    full = lambda a: pl.BlockSpec(a.shape, lambda i: (0, 0))
    return pl.pallas_call(
        functools.partial(_dft_kernel, n_ctx_chunks=dims["m_ctx"] // tch, s_ctx=s_ctx, fg=fg),
        grid=(M // tch,),
        in_specs=[pl.BlockSpec((tch, df), lambda i: (i, 0)), full(chan), full(pos_c), full(pos_l)],
        out_specs=pl.BlockSpec((tch, df), lambda i: (i, 0)),
        out_shape=jax.ShapeDtypeStruct((M, df), BF16),
        compiler_params=_params(1, 48),
        name="dft_mix",
    )(z, chan, pos_c, pos_l)


def _pool(z, pool_scale_l, dims, dp, col_blk):
    M = z.shape[0]
    tch = dims["tch"]
    return pl.pallas_call(
        functools.partial(_pool_kernel, n_ctx_chunks=dims["m_ctx"] // tch, s_ctx=dims["s_ctx"],
                          pg=dp // len(POOL_WINDOWS)),
        grid=(M // tch,),
        in_specs=[pl.BlockSpec((tch, dp), lambda i: (i, col_blk)),
                  pl.BlockSpec((1, dp), lambda i: (0, 0))],
        out_specs=pl.BlockSpec((tch, dp), lambda i: (i, 0)),
        out_shape=jax.ShapeDtypeStruct((M, dp), BF16),
        compiler_params=_params(1, 32),
        name="pool",
    )(z, pool_scale_l.reshape(1, dp))


def _rope_tables(n_tokens):
    rows = n_tokens // GRID_W
    row = np.repeat(np.arange(rows, dtype=np.float64), GRID_W)
    col = np.tile(np.arange(GRID_W, dtype=np.float64), rows)
    inv = ROPE_BASE ** (-np.arange(ROPE_DIM // 4, dtype=np.float64) * (4.0 / ROPE_DIM))
    ar = row[:, None] * inv[None, :]
    ac = col[:, None] * inv[None, :]
    ang = np.concatenate([ar, ar, ac, ac], axis=1)
    sign = np.tile(np.concatenate([-np.ones(ROPE_DIM // 4), np.ones(ROPE_DIM // 4)]), 2)
    cos = np.concatenate([np.cos(ang), np.ones_like(ang)], axis=1)
    sin = np.concatenate([np.sin(ang) * sign[None, :], np.zeros_like(ang)], axis=1)
    return jnp.asarray(cos, F32), jnp.asarray(sin, F32)


def _rope(x, cos, sin):
    q = ROPE_DIM // 4
    lane = jax.lax.broadcasted_iota(jnp.int32, x.shape, 1)
    first = (lane & (2 * q - 1)) < q
    swapped = jnp.where(first, pltpu.roll(x, LANE - q, 1), pltpu.roll(x, q, 1))
    return x * cos + swapped * sin


def _q_kernel(cq_ref, g_ref, w_ref, cos_ref, sin_ref, o_ref, wp_ref, *, n_ctx_tiles):
    i = pl.program_id(0)

    @pl.when(i == 0)
    def _():
        hd = NOPE_DIM + ROPE_DIM
        for h in range(N_HEADS):
            wp_ref[:, h * HEAD_PAD:h * HEAD_PAD + hd] = w_ref[0, :, h * hd:(h + 1) * hd].astype(BF16)
            wp_ref[:, h * HEAD_PAD + hd:(h + 1) * HEAD_PAD] = jnp.zeros((wp_ref.shape[0], HEAD_PAD - hd), BF16)

    def q_rows(rows):
        x = cq_ref[rows, :]
        r = jax.lax.rsqrt(jnp.mean(x * x, axis=-1, keepdims=True) + EPS)
        return _dot((x * r * g_ref[...]).astype(BF16), wp_ref[...]) * Q_SCALE

    @pl.when(i < n_ctx_tiles)
    def _():
        for rows in _slabs(cq_ref.shape[0]):
            o_ref[rows, :] = q_rows(rows).astype(o_ref.dtype)

    @pl.when(i >= n_ctx_tiles)
    def _():
        for rows in _slabs(cq_ref.shape[0]):
            q = q_rows(rows)
            cos, sin = cos_ref[rows, :], sin_ref[rows, :]
            for h in range(N_HEADS):
                lo = h * HEAD_PAD
                o_ref[rows, lo:lo + NOPE_DIM] = q[:, lo:lo + NOPE_DIM].astype(o_ref.dtype)
                pe = q[:, lo + NOPE_DIM:lo + HEAD_PAD]
                o_ref[rows, lo + NOPE_DIM:lo + HEAD_PAD] = _rope(pe, cos, sin).astype(o_ref.dtype)


def _q_proj(z, g_q_l, w_q_b, layer, cos, sin, dims, ql, col_blk):
    M = z.shape[0]
    N = N_HEADS * HEAD_PAD
    bm = min(512, dims["tch"])
    n_ctx_tiles = dims["m_ctx"] // bm
    per_seq = dims["tch"] // bm
    pos_blk = lambda i: (jnp.maximum(i - n_ctx_tiles, 0) % per_seq, 0)
    return pl.pallas_call(
        functools.partial(_q_kernel, n_ctx_tiles=n_ctx_tiles),
        grid=(M // bm,),
        in_specs=[pl.BlockSpec((bm, ql), lambda i: (i, col_blk)),
                  pl.BlockSpec((1, ql), lambda i: (0, 0)),
                  pl.BlockSpec((1,) + w_q_b.shape[1:], lambda i: (layer, 0, 0), pipeline_mode=pl.Buffered(1)),
                  pl.BlockSpec((bm, LANE), pos_blk),
                  pl.BlockSpec((bm, LANE), pos_blk)],
        out_specs=pl.BlockSpec((bm, N), lambda i: (i, 0)),
        out_shape=jax.ShapeDtypeStruct((M, N), BF16),
        scratch_shapes=[pltpu.VMEM((ql, N), BF16)],
        compiler_params=_params(1, 56),
        name="q_proj",
    )(z, g_q_l.reshape(1, ql), w_q_b, cos, sin)


def _kv_kernel(ckv_ref, g_ref, w_ref, kpe_ref, cos_ref, sin_ref, ckv_o, kv_o, kpe_o, *, n_ctx_tiles):
    i = pl.program_id(0)
    for rows in _slabs(ckv_ref.shape[0]):
        x = ckv_ref[rows, :]
        r = jax.lax.rsqrt(jnp.mean(x * x, axis=-1, keepdims=True) + EPS)
        c = x * r * g_ref[...]
        ckv_o[rows, :] = c
        kv_o[rows, :] = _dot(c.astype(BF16), w_ref[...]).astype(kv_o.dtype)
    kpe = kpe_ref[...]
    kpe = jnp.where(jax.lax.broadcasted_iota(jnp.int32, kpe.shape, 1) < ROPE_DIM, kpe, 0.0)

    @pl.when(i < n_ctx_tiles)
    def _():
        kpe_o[...] = kpe.astype(kpe_o.dtype)

    @pl.when(i >= n_ctx_tiles)
    def _():
        kpe_o[...] = _rope(kpe, cos_ref[...], sin_ref[...]).astype(kpe_o.dtype)


def _kv_proj(z, kpe_raw, g_kv_l, w_kv, cos, sin, dims, kvl, ckv_blk):
    M = z.shape[0]
    N = w_kv.shape[1]
    bm = min(512, dims["tch"])
    n_ctx_tiles = dims["m_ctx"] // bm
    per_seq = dims["tch"] // bm
    pos_blk = lambda i: (jnp.maximum(i - n_ctx_tiles, 0) % per_seq, 0)
    return pl.pallas_call(
        functools.partial(_kv_kernel, n_ctx_tiles=n_ctx_tiles),
        grid=(M // bm,),
        in_specs=[pl.BlockSpec((bm, kvl), lambda i: (i, ckv_blk)),
                  pl.BlockSpec((1, kvl), lambda i: (0, 0)),
                  pl.BlockSpec((kvl, N), lambda i: (0, 0)),
                  pl.BlockSpec((bm, LANE), lambda i: (i, 0)),
                  pl.BlockSpec((bm, LANE), pos_blk),
                  pl.BlockSpec((bm, LANE), pos_blk)],
        out_specs=[pl.BlockSpec((bm, kvl), lambda i: (i, 0)),
                   pl.BlockSpec((bm, N), lambda i: (i, 0)),
                   pl.BlockSpec((bm, LANE), lambda i: (i, 0))],
        out_shape=[jax.ShapeDtypeStruct((M, kvl), F32),
                   jax.ShapeDtypeStruct((M, N), BF16),
                   jax.ShapeDtypeStruct((M, LANE), BF16)],
        compiler_params=_params(1, 48),
        name="kv_proj",
    )(z, g_kv_l.reshape(1, kvl), w_kv, kpe_raw, cos, sin)


def _kv_cache_kernel(c_ref, w_ref, o_ref):
    o_ref[...] = _dot(c_ref[0, 0].astype(BF16), w_ref[...]).astype(o_ref.dtype)


def _kv_cache_proj(cache_ckv, w_kv, layer):
    DB, _, P, kvl = cache_ckv.shape
    N = w_kv.shape[1]
    return pl.pallas_call(
        _kv_cache_kernel,
        grid=(DB,),
        in_specs=[pl.BlockSpec((1, 1, P, kvl), lambda b: (b, layer, 0, 0)),
                  pl.BlockSpec((kvl, N), lambda b: (0, 0))],
        out_specs=pl.BlockSpec((P, N), lambda b: (b, 0)),
        out_shape=jax.ShapeDtypeStruct((DB * P, N), BF16),
        compiler_params=_params(1, 32),
        name="kv_cache_proj",
    )(cache_ckv, w_kv)


def _attend(qh, keys, values):
    ss = [_dot_nt(qh, k) for k in keys]
    m = functools.reduce(jnp.maximum, [jnp.max(s, axis=-1, keepdims=True) for s in ss])
    acc = None
    for s, v in zip(ss, values):
        p = jnp.exp2(s - m).astype(BF16)
        part = _dot(p, jnp.concatenate([v, jnp.ones_like(v)], axis=1))
        acc = part if acc is None else acc + part
    return acc[:, :V_DIM] * (1.0 / acc[:, V_DIM:])


def _attn_ctx_kernel(q_ref, kv_ref, kpe_ref, o_ref):
    kpe = kpe_ref[...]
    for h in range(N_HEADS):
        lo = h * HEAD_PAD
        k = jnp.concatenate([kv_ref[:, lo:lo + NOPE_DIM], kpe], axis=1)
        o = _attend(q_ref[:, lo:lo + HEAD_PAD], [k], [kv_ref[:, lo + NOPE_DIM:lo + HEAD_PAD]])
        o_ref[:, h * V_DIM:(h + 1) * V_DIM] = o.astype(o_ref.dtype)


def _attn_ctx(q, kv, kpe, dims):
    M = q.shape[0]
    s = dims["s_ctx"]
    n = dims["m_ctx"] // s
    dv = N_HEADS * V_DIM
    return pl.pallas_call(
        _attn_ctx_kernel,
        grid=(n,),
        in_specs=[pl.BlockSpec((s, q.shape[1]), lambda b: (b, 0)),
                  pl.BlockSpec((s, kv.shape[1]), lambda b: (b, 0)),
                  pl.BlockSpec((s, LANE), lambda b: (b, 0))],
        out_specs=pl.BlockSpec((s, dv), lambda b: (b, 0)),
        out_shape=jax.ShapeDtypeStruct((dims["m_ctx"], dv), BF16),
        compiler_params=_params(1, 32),
        name="attn_ctx",
    )(q, kv, kpe)


def _attn_lat_kernel(q_ref, kv_ref, kpe_ref, kvc_ref, kpec_ref, o_ref):
    kpe = kpe_ref[...]
    kpec = kpec_ref[0, 0].astype(BF16)
    kpec = jnp.concatenate([kpec, jnp.zeros_like(kpec)], axis=1)
    for h in range(N_HEADS):
        lo = h * HEAD_PAD
        qh = q_ref[:, lo:lo + HEAD_PAD]
        k_own = jnp.concatenate([kv_ref[:, lo:lo + NOPE_DIM], kpe], axis=1)
        k_c = jnp.concatenate([kvc_ref[:, lo:lo + NOPE_DIM], kpec], axis=1)
        o = _attend(qh, [k_c, k_own], [kvc_ref[:, lo + NOPE_DIM:lo + HEAD_PAD], kv_ref[:, lo + NOPE_DIM:lo + HEAD_PAD]])
        o_ref[:, h * V_DIM:(h + 1) * V_DIM] = o.astype(o_ref.dtype)


def _attn_lat(q, kv, kpe, kv_cache, cache_krope, layer, dims):
    tch, m_ctx = dims["tch"], dims["m_ctx"]
    DB, _, P, rd = cache_krope.shape
    bq = min(256, tch)
    nq = tch // bq
    dv = N_HEADS * V_DIM
    q_blk = lambda b, t: (m_ctx // bq + b * nq + t, 0)
    own_blk = lambda b, t: (m_ctx // tch + b, 0)
    return pl.pallas_call(
        _attn_lat_kernel,
        grid=(DB, nq),
        in_specs=[pl.BlockSpec((bq, q.shape[1]), q_blk),
                  pl.BlockSpec((tch, kv.shape[1]), own_blk),
                  pl.BlockSpec((tch, LANE), own_blk),
                  pl.BlockSpec((P, kv_cache.shape[1]), lambda b, t: (b, 0)),
                  pl.BlockSpec((1, 1, P, rd), lambda b, t: (b, layer, 0, 0))],
        out_specs=pl.BlockSpec((bq, dv), lambda b, t: (b * nq + t, 0)),
        out_shape=jax.ShapeDtypeStruct((DB * tch, dv), BF16),
        compiler_params=_params(2, 48),
        name="attn_lat",
    )(q, kv, kpe, kv_cache, cache_krope)


def kernel(x_prompt, x_sample, cache_ckv, cache_krope, c, c_ctx, w_ada, b_ada, g_norm1, g_norm2, w_in, w_fourier,
           pool_scale, w_pool, g_q, w_q_b, g_kv, w_kv_b, w_o_mla, w_out, w_gate, w_up, w_down, g_final):
    B, S, D = x_prompt.shape
    DB, DS, _ = x_sample.shape
    L = w_ada.shape[0]
    DF = w_fourier.shape[1]
    DP = pool_scale.shape[1]
    QL = g_q.shape[1]
    KVL = g_kv.shape[1]
    m_ctx, m_lat = B * S, DB * DS
    M = m_ctx + m_lat
    assert DS % S == 0 and m_ctx % DS == 0 and S & (S - 1) == 0 and DS & (DS - 1) == 0
    assert 1 + DB <= MOD_ROWS and DS % GRID_W == 0
    dims = dict(m_ctx=m_ctx, tch=DS, s_ctx=S)

    off_p, off_q, off_kv, off_pe = DF, DF + DP, DF + DP + QL, DF + DP + QL + KVL
    assert off_p % DP == 0 and off_q % QL == 0 and off_kv % KVL == 0 and off_pe % IN_BN == 0

    xs = (x_prompt.reshape(m_ctx, D), x_sample.reshape(m_lat, D))
    cond = jnp.zeros((MOD_ROWS, D), F32).at[0].set(c_ctx).at[1:1 + DB].set(c)
    b_ada3 = b_ada.reshape(L, 1, 6 * D)
    mod4 = _ada_mod(cond, w_ada, b_ada3, 0).reshape(1, MOD_ROWS, 1, 6 * D)
    cos, sin = _rope_tables(DS)
    w_in_t = jnp.swapaxes(w_in, 1, 2)
    wf_b, wp_b, wo_b = w_fourier.astype(BF16), w_pool.reshape(L, DP, D).astype(BF16), w_o_mla.astype(BF16)
    w_out_b = w_out.astype(BF16)

    ckv_layers, kpe_layers = [], []
    for l in range(L):
        w_kv = w_kv_b[l].astype(BF16)

        h = _norm_mod(xs, g_norm1[l:l + 1], mod4, 0, dims)
        z, kpe_raw, gates = _in_proj(h, w_in_t, l, off_pe, off_pe + ROPE_DIM, 2048)

        mixed = _dft(z, dims, DF)
        pooled = _pool(z, pool_scale[l], dims, DP, off_p // DP)
        q = _q_proj(z, g_q[l], w_q_b, l, cos, sin, dims, QL, off_q // QL)
        c_kv, kv, kpe = _kv_proj(z, kpe_raw, g_kv[l], w_kv, cos, sin, dims, KVL, off_kv // KVL)
        kv_cache = _kv_cache_proj(cache_ckv, w_kv, l)
        o_ctx = _attn_ctx(q, kv, kpe, dims)
        o_lat = _attn_lat(q, kv, kpe, kv_cache, cache_krope, l, dims)

        merged = _merge(mixed, pooled, (o_ctx, o_lat), wf_b, wp_b, wo_b, gates, l, 1024, 1024)
        x = _mm_resid(merged, w_out_b, xs, mod4, l, 2, 1024, 1024, dims, 60, "out_proj")

        h = _norm_mod((x,), g_norm2[l:l + 1], mod4, 3, dims)
        act, w_down_b = _swiglu_up(h, w_gate, w_up, w_down, l, 2048, 256)
        if l + 1 < L:
            x, mod_next = _mm_resid(act, w_down_b[None], (x,), mod4, 0, 5, 512, 512, dims, 60, "down_proj",
                                    ada_side=(cond, w_ada, b_ada3, l + 1))
            mod4 = mod_next.reshape(1, MOD_ROWS, 1, 6 * D)
        else:
            x = _mm_resid(act, w_down_b[None], (x,), mod4, 0, 5, 512, 512, dims, 56, "down_proj")
        xs = (x,)

        ckv_layers.append(c_kv[:m_ctx].reshape(B, S, KVL))
        kpe_layers.append(kpe_raw[:m_ctx, :ROPE_DIM].reshape(B, S, ROPE_DIM))

    y_prompt = _final_norm(x, g_final, 0, m_ctx).reshape(B, S, D)
    y_sample = _final_norm(x, g_final, m_ctx, m_lat).reshape(DB, DS, D)
    return (y_prompt, y_sample, jnp.stack(ckv_layers, axis=1), jnp.stack(kpe_layers, axis=1))
```
